```python
import jax, jax.numpy as jnp
from jax import lax
import numpy as np

D_MODEL = 2048
BATCH = 8
SEQ = 4096
DEPTH = 1
DEC_BATCH = 4
DEC_SEQ = 2048
PAST_LEN = 128

HEAD_DIM = 128
A_Q_HEADS = 8
A_KV_HEADS = 2
A_GROUP = A_Q_HEADS // A_KV_HEADS
WINDOW = 128
BLK = WINDOW
N_BUCKETS = 32
MAX_DISTANCE = 128
B_HEADS = 8
GRID_W = 64
NA_KH = 8
NA_KW = 16
NA_QCB = NA_KW
NA_REG = 2 * NA_KW
D_FF = 4 * D_MODEL
EPS = 1e-6
NEG = -1e30

A_Q_W = A_Q_HEADS * HEAD_DIM
A_KV_W = A_KV_HEADS * HEAD_DIM
B_W = B_HEADS * HEAD_DIM
IN_SPLITS = (A_Q_W, A_KV_W, A_KV_W, B_W, B_W, B_W, D_MODEL, D_MODEL)
IN_WIDTH = A_Q_W + 2 * A_KV_W + 3 * B_W + 2 * D_MODEL

kernel_name = "hybrid_window_gqa_neighbourhood_encoder"


def rmsnorm(x, g):
    xf = x.astype(jnp.float32)
    y = xf * lax.rsqrt(jnp.mean(xf * xf, axis=-1, keepdims=True) + EPS) * g.astype(jnp.float32)
    return y.astype(x.dtype)


def t5_bucket(rel):
    nb = N_BUCKETS // 2
    max_exact = nb // 2
    ret = (rel > 0).astype(np.int32) * nb
    n = np.abs(rel).astype(np.int32)
    nf = np.maximum(n, max_exact).astype(np.float32)
    large = max_exact + (np.log(nf / max_exact) / np.log(MAX_DISTANCE / max_exact) * (nb - max_exact)).astype(np.int32)
    large = np.minimum(large, nb - 1)
    return ret + np.where(n < max_exact, n, large)


def window_attention(q, k, v, q_gain, k_gain, t5_table, sink):
    b, L = q.shape[0], q.shape[1]
    nb = L // BLK
    q = rmsnorm(q, q_gain) * (HEAD_DIM ** -0.5)
    k = rmsnorm(k, k_gain)
    qb = q.reshape(b, nb, BLK, A_KV_HEADS, A_GROUP, HEAD_DIM)

    def band(t):
        tp = jnp.pad(t, ((0, 0), (BLK, BLK), (0, 0), (0, 0))).reshape(b, nb + 2, BLK, A_KV_HEADS, HEAD_DIM)
        return jnp.concatenate([tp[:, :-2], tp[:, 1:-1], tp[:, 2:]], axis=2)

    kb, vb = band(k), band(v)
    s = jnp.einsum('bnqhgd,bnkhd->bnhgqk', qb, kb).astype(jnp.float32)
    rel = np.arange(3 * BLK)[None, :] - BLK - np.arange(BLK)[:, None]
    bias = t5_table[t5_bucket(rel)].astype(jnp.float32)
    bias = jnp.transpose(bias, (2, 0, 1)).reshape(A_KV_HEADS, A_GROUP, BLK, 3 * BLK)
    kpos = np.arange(nb)[:, None] * BLK - BLK + np.arange(3 * BLK)[None, :]
    mask = (np.abs(rel) <= WINDOW)[None] & ((kpos >= 0) & (kpos < L))[:, None, :]
    s = jnp.where(mask[None, :, None, None], s + bias[None, None], NEG)
    sk = sink.astype(jnp.float32).reshape(1, 1, A_KV_HEADS, A_GROUP, 1, 1)
    m = jnp.maximum(jnp.max(s, axis=-1, keepdims=True), sk)
    p = jnp.exp(s - m)
    denom = jnp.sum(p, axis=-1, keepdims=True) + jnp.exp(sk - m)
    o = jnp.einsum('bnhgqk,bnkhd->bnqhgd', (p / denom).astype(v.dtype), vb)
    return o.reshape(b, L, A_Q_W)


def neighbourhood_attention(q, k, v, q_gain, k_gain, rpb):
    b, L = q.shape[0], q.shape[1]
    rows = L // GRID_W
    kh = min(NA_KH, rows)
    ncb = GRID_W // NA_QCB
    q = rmsnorm(q, q_gain) * (HEAD_DIM ** -0.5)
    k = rmsnorm(k, k_gain)
    qg = q.reshape(b, rows, GRID_W, B_HEADS, HEAD_DIM)
    kg = k.reshape(b, rows, GRID_W, B_HEADS, HEAD_DIM)
    vg = v.reshape(b, rows, GRID_W, B_HEADS, HEAD_DIM)
    reg_start = np.clip(np.arange(ncb) * NA_QCB - NA_KW // 2, 0, GRID_W - NA_REG)
    col_idx = reg_start[:, None] + np.arange(NA_REG)[None, :]
    qc = np.arange(GRID_W).reshape(ncb, NA_QCB)
    start_c = np.clip(qc - NA_KW // 2, 0, GRID_W - NA_KW)
    kc = col_idx[:, None, :]
    col_mask = (kc >= start_c[:, :, None]) & (kc < start_c[:, :, None] + NA_KW)
    dcc = np.clip(kc - qc[:, :, None], -(NA_KW - 1), NA_KW - 1) + (NA_KW - 1)

    def row_fn(r):
        rs = jnp.clip(r - kh // 2, 0, rows - kh)
        kr = lax.dynamic_slice_in_dim(kg, rs, kh, axis=1)[:, :, col_idx]
        vr = lax.dynamic_slice_in_dim(vg, rs, kh, axis=1)[:, :, col_idx]
        qr = lax.dynamic_index_in_dim(qg, r, axis=1, keepdims=False).reshape(b, ncb, NA_QCB, B_HEADS, HEAD_DIM)
        s = jnp.einsum('bcqhd,brckhd->bhcqrk', qr, kr).astype(jnp.float32)
        dr = rs + jnp.arange(kh) - r + (NA_KH - 1)
        bias = rpb[:, dr][:, :, dcc].astype(jnp.float32)
        bias = jnp.transpose(bias, (0, 2, 3, 1, 4))
        s = jnp.where(col_mask[None, None, :, :, None, :], s + bias[None], NEG)
        p = jax.nn.softmax(s.reshape(b, B_HEADS, ncb, NA_QCB, kh * NA_REG), axis=-1).reshape(s.shape)
        o = jnp.einsum('bhcqrk,brckhd->bcqhd', p.astype(v.dtype), vr)
        return o.reshape(b, GRID_W, B_W)

    out = lax.map(row_fn, jnp.arange(rows))
    return jnp.transpose(out, (1, 0, 2, 3)).reshape(b, L, B_W)


def trunk(x, norm_mix_g, w_in, q_norm_a, k_norm_a, t5_bias, sink_a, q_norm_b, k_norm_b, rpb_b,
          w_br_a, w_br_b, w_out, norm_mlp_g, w_up, w_down):
    b, L = x.shape[0], x.shape[1]
    offs = list(np.cumsum(IN_SPLITS)[:-1])
    for l in range(DEPTH):
        h = rmsnorm(x, norm_mix_g[l])
        proj = h @ w_in[l]
        qa, ka, va, qb, kb, vb, ga, gb = jnp.split(proj, offs, axis=-1)
        ya = window_attention(qa.reshape(b, L, A_Q_HEADS, HEAD_DIM),
                              ka.reshape(b, L, A_KV_HEADS, HEAD_DIM),
                              va.reshape(b, L, A_KV_HEADS, HEAD_DIM),
                              q_norm_a[l], k_norm_a[l], t5_bias, sink_a[l]) @ w_br_a[l]
        yb = neighbourhood_attention(qb.reshape(b, L, B_HEADS, HEAD_DIM),
                                     kb.reshape(b, L, B_HEADS, HEAD_DIM),
                                     vb.reshape(b, L, B_HEADS, HEAD_DIM),
                                     q_norm_b[l], k_norm_b[l], rpb_b[l]) @ w_br_b[l]
        merged = jax.nn.sigmoid(ga) * ya + jax.nn.sigmoid(gb) * yb
        x = x + merged @ w_out[l]
        h = rmsnorm(x, norm_mlp_g[l])
        x = x + jnp.square(jax.nn.relu(h @ w_up[l])) @ w_down[l]
    return x


def setup_inputs(seed: int = 0) -> dict:
    key = jax.random.key(seed)
    ks = jax.random.split(key, 20)
    f = jnp.float32
    n = lambda k, shape, s: jax.random.normal(k, shape, f) * s
    return {
        "x_prompt": n(ks[0], (BATCH, SEQ, D_MODEL), 1.0),
        "x_sample": n(ks[1], (DEC_BATCH, DEC_SEQ, D_MODEL), 1.0),
        "norm_mix_g": 1.0 + n(ks[2], (DEPTH, D_MODEL), 0.1),
        "w_in": n(ks[3], (DEPTH, D_MODEL, IN_WIDTH), D_MODEL ** -0.5),
        "q_norm_a": 1.0 + n(ks[4], (DEPTH, HEAD_DIM), 0.1),
        "k_norm_a": 1.0 + n(ks[5], (DEPTH, HEAD_DIM), 0.1),
        "t5_bias": n(ks[6], (N_BUCKETS, A_Q_HEADS), 0.5),
        "sink_a": n(ks[7], (DEPTH, A_Q_HEADS), 1.0),
        "q_norm_b": 1.0 + n(ks[8], (DEPTH, HEAD_DIM), 0.1),
        "k_norm_b": 1.0 + n(ks[9], (DEPTH, HEAD_DIM), 0.1),
        "rpb_b": n(ks[10], (DEPTH, B_HEADS, 2 * NA_KH - 1, 2 * NA_KW - 1), 0.5),
        "w_br_a": n(ks[11], (DEPTH, A_Q_W, D_MODEL), A_Q_W ** -0.5),
        "w_br_b": n(ks[12], (DEPTH, B_W, D_MODEL), B_W ** -0.5),
        "w_out": n(ks[13], (DEPTH, D_MODEL, D_MODEL), D_MODEL ** -0.5),
        "norm_mlp_g": 1.0 + n(ks[14], (DEPTH, D_MODEL), 0.1),
        "w_up": n(ks[15], (DEPTH, D_MODEL, D_FF), D_MODEL ** -0.5),
        "w_down": n(ks[16], (DEPTH, D_FF, D_MODEL), D_FF ** -0.5),
    }


def reference(x_prompt, x_sample, norm_mix_g, w_in, q_norm_a, k_norm_a, t5_bias, sink_a, q_norm_b, k_norm_b,
              rpb_b, w_br_a, w_br_b, w_out, norm_mlp_g, w_up, w_down):
    y_prompt = trunk(x_prompt, norm_mix_g, w_in, q_norm_a, k_norm_a, t5_bias, sink_a, q_norm_b, k_norm_b,
                     rpb_b, w_br_a, w_br_b, w_out, norm_mlp_g, w_up, w_down)
    y_sample = trunk(x_sample, norm_mix_g, w_in, q_norm_a, k_norm_a, t5_bias, sink_a, q_norm_b, k_norm_b,
                     rpb_b, w_br_a, w_br_b, w_out, norm_mlp_g, w_up, w_down)
    return (y_prompt, y_sample)
```

```python
import functools

import jax
import jax.numpy as jnp
import numpy as np
from jax import lax
from jax.experimental import pallas as pl
from jax.experimental.pallas import tpu as pltpu

D_MODEL = 2048
HEAD_DIM = 128
A_Q_HEADS = 8
A_KV_HEADS = 2
A_GROUP = A_Q_HEADS // A_KV_HEADS
WINDOW = 128
BLK = WINDOW
N_BUCKETS = 32
MAX_DISTANCE = 128
B_HEADS = 8
GRID_W = 64
NA_KH = 8
NA_KW = 16
D_FF = 4 * D_MODEL
EPS = 1e-6
NEG = -1e30

A_Q_W = A_Q_HEADS * HEAD_DIM
A_KV_W = A_KV_HEADS * HEAD_DIM
B_W = B_HEADS * HEAD_DIM
IN_WIDTH = A_Q_W + 2 * A_KV_W + 3 * B_W + 2 * D_MODEL

COL_QA = 0
COL_QB = COL_QA + A_Q_W
COL_KB = COL_QB + B_W
COL_VB = COL_KB + B_W
COL_GA = COL_VB + B_W
COL_GB = COL_GA + D_MODEL
COL_KVA = COL_GB + D_MODEL
assert COL_KVA + 2 * A_KV_W == IN_WIDTH

V7X_VMEM_BYTES = 64 * 1024 * 1024
VMEM_LIMIT_BYTES = V7X_VMEM_BYTES - 8 * 1024 * 1024

PROJ_TM = 1024
PROJ_TN = 512
N_NORMED_TILES = COL_VB // PROJ_TN
KVA_TILE = COL_KVA // PROJ_TN
WIN_TQ = 4 * BLK
MERGE_TM = 256
MLP_TM = 1024
MLP_TF = 512

BF16 = jnp.bfloat16
F32 = jnp.float32


def _rms_scale(x):
    return lax.rsqrt(jnp.mean(x * x, axis=-1, keepdims=True) + EPS)


def _head_rmsnorm(a):
    parts = []
    for c in range(a.shape[1] // HEAD_DIM):
        blk = a[:, c * HEAD_DIM:(c + 1) * HEAD_DIM]
        parts.append(blk * _rms_scale(blk))
    return jnp.concatenate(parts, axis=1)


def _in_proj_kernel(x_ref, g_ref, w_ref, cg_ref, o_ref, h_ref):
    j = pl.program_id(1)

    @pl.when(j == 0)
    def _():
        x = x_ref[...]
        h_ref[...] = (x * _rms_scale(x) * g_ref[...]).astype(BF16)

    acc = jnp.dot(h_ref[...], w_ref[...], preferred_element_type=F32)

    @pl.when(j < N_NORMED_TILES)
    def _():
        o_ref[...] = (_head_rmsnorm(acc) * cg_ref[...]).astype(BF16)

    @pl.when(j == KVA_TILE)
    def _():
        k = _head_rmsnorm(acc[:, :A_KV_W]) * cg_ref[:, :A_KV_W]
        o_ref[...] = jnp.concatenate([k, acc[:, A_KV_W:]], axis=1).astype(BF16)

    @pl.when(jnp.logical_and(j >= N_NORMED_TILES, j != KVA_TILE))
    def _():
        o_ref[...] = acc.astype(BF16)


def _in_proj(x2, g, w, colgain):
    m = x2.shape[0]
    return pl.pallas_call(
        _in_proj_kernel,
        grid=(m // PROJ_TM, IN_WIDTH // PROJ_TN),
        in_specs=[
            pl.BlockSpec((PROJ_TM, D_MODEL), lambda i, j: (i, 0)),
            pl.BlockSpec((1, D_MODEL), lambda i, j: (0, 0)),
            pl.BlockSpec((D_MODEL, PROJ_TN), lambda i, j: (0, j)),
            pl.BlockSpec((1, PROJ_TN), lambda i, j: (0, j)),
        ],
        out_specs=pl.BlockSpec((PROJ_TM, PROJ_TN), lambda i, j: (i, j)),
        out_shape=jax.ShapeDtypeStruct((m, IN_WIDTH), BF16),
        scratch_shapes=[pltpu.VMEM((PROJ_TM, D_MODEL), BF16)],
        compiler_params=pltpu.CompilerParams(
            dimension_semantics=("parallel", "arbitrary"), vmem_limit_bytes=VMEM_LIMIT_BYTES),
        name="in_proj",
    )(x2, g, w, colgain)


def _win_attn_kernel(sink_ref, q_ref, kvp_ref, kvm_ref, kvn_ref, bias_ref, o_ref, *, n_tiles):
    i = pl.program_id(1)
    kv = jnp.concatenate([kvp_ref[0], kvm_ref[0], kvn_ref[0]], axis=0)
    col = lax.broadcasted_iota(jnp.int32, (1, 3 * BLK), 1)
    first_key = jnp.where(i == 0, BLK, 0)
    end_key = jnp.where(i == n_tiles - 1, 2 * BLK, 3 * BLK)
    n_blk = WIN_TQ // BLK
    for t in range(n_blk):
        for kvh in range(A_KV_HEADS):
            heads = [kvh * A_GROUP + g for g in range(A_GROUP)]
            k = kv[t * BLK:(t + 3) * BLK, kvh * HEAD_DIM:(kvh + 1) * HEAD_DIM]
            v = kv[t * BLK:(t + 3) * BLK, A_KV_W + kvh * HEAD_DIM:A_KV_W + (kvh + 1) * HEAD_DIM]
            q = jnp.concatenate(
                [q_ref[0, t * BLK:(t + 1) * BLK, h * HEAD_DIM:(h + 1) * HEAD_DIM] for h in heads], axis=0)
            s = lax.dot_general(q, k, (((1,), (1,)), ((), ())), preferred_element_type=F32)
            s = s + bias_ref[kvh * A_GROUP:(kvh + 1) * A_GROUP].reshape(A_GROUP * BLK, 3 * BLK)
            if t == 0:
                s = jnp.where(col >= first_key, s, NEG)
            if t == n_blk - 1:
                s = jnp.where(col < end_key, s, NEG)
            sink = jnp.concatenate([jnp.full((BLK, 1), sink_ref[h], F32) for h in heads], axis=0)
            m = jnp.maximum(jnp.max(s, axis=-1, keepdims=True), sink)
            p = jnp.exp(s - m)
            denom = jnp.sum(p, axis=-1, keepdims=True) + jnp.exp(sink - m)
            o = jnp.dot(p.astype(BF16), v, preferred_element_type=F32) / denom
            for g, h in enumerate(heads):
                o_ref[0, t * BLK:(t + 1) * BLK, h * HEAD_DIM:(h + 1) * HEAD_DIM] = (
                    o[g * BLK:(g + 1) * BLK].astype(BF16))


def _win_attn(proj3, sink, bias_a):
    b, seq, _ = proj3.shape
    n_tiles = seq // WIN_TQ
    blk_per_tile = WIN_TQ // BLK
    last_blk = seq // BLK - 1
    kva_wide = COL_KVA // (2 * A_KV_W)
    return pl.pallas_call(
        functools.partial(_win_attn_kernel, n_tiles=n_tiles),
        grid=(b, n_tiles),
        in_specs=[
            pl.BlockSpec(memory_space=pltpu.SMEM),
            pl.BlockSpec((1, WIN_TQ, A_Q_W), lambda bi, i: (bi, i, COL_QA // A_Q_W)),
            pl.BlockSpec((1, BLK, 2 * A_KV_W),
                         lambda bi, i: (bi, jnp.maximum(i * blk_per_tile - 1, 0), kva_wide)),
            pl.BlockSpec((1, WIN_TQ, 2 * A_KV_W), lambda bi, i: (bi, i, kva_wide)),
            pl.BlockSpec((1, BLK, 2 * A_KV_W),
                         lambda bi, i: (bi, jnp.minimum((i + 1) * blk_per_tile, last_blk), kva_wide)),
            pl.BlockSpec((A_Q_HEADS, BLK, 3 * BLK), lambda bi, i: (0, 0, 0)),
        ],
        out_specs=pl.BlockSpec((1, WIN_TQ, A_Q_W), lambda bi, i: (bi, i, 0)),
        out_shape=jax.ShapeDtypeStruct((b, seq, A_Q_W), BF16),
        compiler_params=pltpu.CompilerParams(
            dimension_semantics=("parallel", "arbitrary"), vmem_limit_bytes=VMEM_LIMIT_BYTES),
        name="win_attn",
    )(sink, proj3, proj3, proj3, proj3, bias_a)


NBR_UNROLL = 4


def _nbr_attn_kernel(q_ref, k_ref, v_ref, bias_ref, o_ref, *, rows):
    def body(r, carry):
        rs = jnp.clip(r - NA_KH // 2, 0, rows - NA_KH)
        q0 = pl.multiple_of(r * GRID_W, GRID_W)
        k0 = pl.multiple_of(rs * GRID_W, GRID_W)
        q = q_ref[0, pl.ds(q0, GRID_W), :]
        k = k_ref[0, pl.ds(k0, NA_KH * GRID_W), :]
        v = v_ref[0, pl.ds(k0, NA_KH * GRID_W), :]
        s = lax.dot_general(q, k, (((1,), (1,)), ((), ())), preferred_element_type=F32)
        s = s + bias_ref[0, r - rs]
        m = jnp.max(s, axis=-1, keepdims=True)
        p = jnp.exp(s - m)
        denom = jnp.sum(p, axis=-1, keepdims=True)
        o = jnp.dot(p.astype(BF16), v, preferred_element_type=F32) / denom
        o_ref[0, pl.ds(q0, GRID_W), :] = o.astype(BF16)
        return carry

    lax.fori_loop(0, rows, body, 0, unroll=NBR_UNROLL)


def _nbr_attn(proj3, bias_b):
    b, seq, _ = proj3.shape
    rows = seq // GRID_W
    return pl.pallas_call(
        functools.partial(_nbr_attn_kernel, rows=rows),
        grid=(b, B_HEADS),
        in_specs=[
            pl.BlockSpec((1, seq, HEAD_DIM), lambda bi, h: (bi, 0, COL_QB // HEAD_DIM + h)),
            pl.BlockSpec((1, seq, HEAD_DIM), lambda bi, h: (bi, 0, COL_KB // HEAD_DIM + h)),
            pl.BlockSpec((1, seq, HEAD_DIM), lambda bi, h: (bi, 0, COL_VB // HEAD_DIM + h)),
            pl.BlockSpec((1, NA_KH, GRID_W, NA_KH * GRID_W), lambda bi, h: (h, 0, 0, 0)),
        ],
        out_specs=pl.BlockSpec((1, seq, HEAD_DIM), lambda bi, h: (bi, 0, h)),
        out_shape=jax.ShapeDtypeStruct((b, seq, B_W), BF16),
        compiler_params=pltpu.CompilerParams(
            dimension_semantics=("parallel", "arbitrary"), vmem_limit_bytes=VMEM_LIMIT_BYTES),
        name="nbr_attn",
    )(proj3, proj3, proj3, bias_b)


def _merge_kernel(oa_ref, ob_ref, ga_ref, gb_ref, x_ref, wa_ref, wb_ref, wo_ref, o_ref):
    ya = jnp.dot(oa_ref[...], wa_ref[...], preferred_element_type=F32)
    yb = jnp.dot(ob_ref[...], wb_ref[...], preferred_element_type=F32)
    merged = jax.nn.sigmoid(ga_ref[...].astype(F32)) * ya + jax.nn.sigmoid(gb_ref[...].astype(F32)) * yb
    o_ref[...] = x_ref[...] + jnp.dot(merged.astype(BF16), wo_ref[...], preferred_element_type=F32)


def _merge(oa, ob, proj, x2, wa, wb, wo):
    m = x2.shape[0]
    resident = functools.partial(pl.BlockSpec, index_map=lambda i: (0, 0), pipeline_mode=pl.Buffered(1))
    return pl.pallas_call(
        _merge_kernel,
        grid=(m // MERGE_TM,),
        in_specs=[
            pl.BlockSpec((MERGE_TM, A_Q_W), lambda i: (i, 0)),
            pl.BlockSpec((MERGE_TM, B_W), lambda i: (i, 0)),
            pl.BlockSpec((MERGE_TM, D_MODEL), lambda i: (i, COL_GA // D_MODEL)),
            pl.BlockSpec((MERGE_TM, D_MODEL), lambda i: (i, COL_GB // D_MODEL)),
            pl.BlockSpec((MERGE_TM, D_MODEL), lambda i: (i, 0)),
            resident((A_Q_W, D_MODEL)),
            resident((B_W, D_MODEL)),
            resident((D_MODEL, D_MODEL)),
        ],
        out_specs=pl.BlockSpec((MERGE_TM, D_MODEL), lambda i: (i, 0)),
        out_shape=jax.ShapeDtypeStruct((m, D_MODEL), F32),
        compiler_params=pltpu.CompilerParams(
            dimension_semantics=("parallel",), vmem_limit_bytes=VMEM_LIMIT_BYTES),
        name="merge",
    )(oa, ob, proj, proj, x2, wa, wb, wo)


def _mlp_kernel(x_ref, g_ref, wu_ref, wd_ref, o_ref, h_ref):
    @pl.when(pl.program_id(1) == 0)
    def _():
        x = x_ref[...]
        h_ref[...] = (x * _rms_scale(x) * g_ref[...]).astype(BF16)
        o_ref[...] = x

    u = jnp.dot(h_ref[...], wu_ref[...], preferred_element_type=F32)
    a = jnp.square(jnp.maximum(u, 0.0)).astype(BF16)
    o_ref[...] += jnp.dot(a, wd_ref[...], preferred_element_type=F32)


def _mlp(x2, g, wu, wd):
    m = x2.shape[0]
    return pl.pallas_call(
        _mlp_kernel,
        grid=(m // MLP_TM, D_FF // MLP_TF),
        in_specs=[
            pl.BlockSpec((MLP_TM, D_MODEL), lambda i, f: (i, 0)),
            pl.BlockSpec((1, D_MODEL), lambda i, f: (0, 0)),
            pl.BlockSpec((D_MODEL, MLP_TF), lambda i, f: (0, f)),
            pl.BlockSpec((MLP_TF, D_MODEL), lambda i, f: (f, 0)),
        ],
        out_specs=pl.BlockSpec((MLP_TM, D_MODEL), lambda i, f: (i, 0)),
        out_shape=jax.ShapeDtypeStruct((m, D_MODEL), F32),
        scratch_shapes=[pltpu.VMEM((MLP_TM, D_MODEL), BF16)],
        compiler_params=pltpu.CompilerParams(
            dimension_semantics=("parallel", "arbitrary"), vmem_limit_bytes=VMEM_LIMIT_BYTES),
        name="mlp",
    )(x2, g, wu, wd)


def _t5_bucket(rel):
    nb = N_BUCKETS // 2
    max_exact = nb // 2
    ret = (rel > 0).astype(np.int32) * nb
    n = np.abs(rel).astype(np.int32)
    nf = np.maximum(n, max_exact).astype(np.float32)
    large = max_exact + (np.log(nf / max_exact) / np.log(MAX_DISTANCE / max_exact) * (nb - max_exact)).astype(np.int32)
    large = np.minimum(large, nb - 1)
    return ret + np.where(n < max_exact, n, large)


def _window_bias_table(t5_bias):
    rel = np.arange(3 * BLK)[None, :] - BLK - np.arange(BLK)[:, None]
    bias = jnp.transpose(t5_bias[_t5_bucket(rel)].astype(F32), (2, 0, 1))
    return jnp.where((np.abs(rel) <= WINDOW)[None], bias, NEG)


def _nbr_bias_table(rpb):
    qc = np.arange(GRID_W)[:, None]
    kc = np.arange(GRID_W)[None, :]
    start_c = np.clip(qc - NA_KW // 2, 0, GRID_W - NA_KW)
    col_mask = (kc >= start_c) & (kc < start_c + NA_KW)
    dcc = np.clip(kc - qc, -(NA_KW - 1), NA_KW - 1) + (NA_KW - 1)
    d = np.arange(NA_KH)[:, None]
    j = np.arange(NA_KH)[None, :]
    dr = j - d + (NA_KH - 1)
    bias = rpb.astype(F32)[:, dr[:, :, None, None], dcc[None, None, :, :]]
    bias = jnp.where(col_mask[None, None, None], bias, NEG)
    return jnp.transpose(bias, (0, 1, 3, 2, 4)).reshape(B_HEADS, NA_KH, GRID_W, NA_KH * GRID_W)


def _prepare(norm_mix_g, w_in, q_norm_a, k_norm_a, t5_bias, sink_a, q_norm_b, k_norm_b, rpb_b,
             w_br_a, w_br_b, w_out, norm_mlp_g, w_up, w_down):
    w = w_in[0]
    o_qa, o_ka, o_va, o_qb, o_kb, o_vb, o_ga, o_gb = np.cumsum(
        (0, A_Q_W, A_KV_W, A_KV_W, B_W, B_W, B_W, D_MODEL))
    w_perm = jnp.concatenate([
        w[:, o_qa:o_ka], w[:, o_qb:o_kb], w[:, o_kb:o_vb], w[:, o_vb:o_ga], w[:, o_ga:o_gb], w[:, o_gb:],
        w[:, o_ka:o_va], w[:, o_va:o_qb]], axis=1).astype(BF16)
    scale = HEAD_DIM ** -0.5
    ones = jnp.ones((IN_WIDTH - COL_VB - 2 * A_KV_W,), F32)
    colgain = jnp.concatenate([
        jnp.tile(q_norm_a[0].astype(F32), A_Q_HEADS) * scale,
        jnp.tile(q_norm_b[0].astype(F32), B_HEADS) * scale,
        jnp.tile(k_norm_b[0].astype(F32), B_HEADS),
        ones,
        jnp.tile(k_norm_a[0].astype(F32), A_KV_HEADS),
        jnp.ones((A_KV_W,), F32)]).reshape(1, IN_WIDTH)
    return dict(
        g_mix=norm_mix_g[0].astype(F32).reshape(1, D_MODEL),
        w_in=w_perm,
        colgain=colgain,
        sink=sink_a[0].astype(F32),
        bias_a=_window_bias_table(t5_bias),
        bias_b=_nbr_bias_table(rpb_b[0]),
        w_a=w_br_a[0].astype(BF16),
        w_b=w_br_b[0].astype(BF16),
        w_o=w_out[0].astype(BF16),
        g_mlp=norm_mlp_g[0].astype(F32).reshape(1, D_MODEL),
        w_up=w_up[0].astype(BF16),
        w_down=w_down[0].astype(BF16),
    )


def _layer(x, p):
    b, seq, _ = x.shape
    x2 = x.reshape(b * seq, D_MODEL)
    proj = _in_proj(x2, p["g_mix"], p["w_in"], p["colgain"])
    proj3 = proj.reshape(b, seq, IN_WIDTH)
    oa = _win_attn(proj3, p["sink"], p["bias_a"]).reshape(b * seq, A_Q_W)
    ob = _nbr_attn(proj3, p["bias_b"]).reshape(b * seq, B_W)
    x1 = _merge(oa, ob, proj, x2, p["w_a"], p["w_b"], p["w_o"])
    y = _mlp(x1, p["g_mlp"], p["w_up"], p["w_down"])
    return y.reshape(b, seq, D_MODEL)


def kernel(x_prompt, x_sample, norm_mix_g, w_in, q_norm_a, k_norm_a, t5_bias, sink_a, q_norm_b, k_norm_b, rpb_b,
           w_br_a, w_br_b, w_out, norm_mlp_g, w_up, w_down):
    p = _prepare(norm_mix_g, w_in, q_norm_a, k_norm_a, t5_bias, sink_a, q_norm_b, k_norm_b, rpb_b,
                 w_br_a, w_br_b, w_out, norm_mlp_g, w_up, w_down)
    return (_layer(x_prompt, p), _layer(x_sample, p))
```

```python
import functools

import jax
import jax.numpy as jnp
import numpy as np
from jax import lax
from jax.experimental import pallas as pl
from jax.experimental.pallas import tpu as pltpu

D_MODEL = 2048
HEAD_DIM = 128
A_Q_HEADS = 8
A_KV_HEADS = 2
A_GROUP = A_Q_HEADS // A_KV_HEADS
WINDOW = 128
BLK = WINDOW
N_BUCKETS = 32
MAX_DISTANCE = 128
B_HEADS = 8
GRID_W = 64
NA_KH = 8
NA_KW = 16
D_FF = 4 * D_MODEL
EPS = 1e-6
NEG = -1e30

A_Q_W = A_Q_HEADS * HEAD_DIM
A_KV_W = A_KV_HEADS * HEAD_DIM
B_W = B_HEADS * HEAD_DIM
IN_WIDTH = A_Q_W + 2 * A_KV_W + 3 * B_W + 2 * D_MODEL

COL_QA = 0
COL_QB = COL_QA + A_Q_W
COL_KB = COL_QB + B_W
COL_VB = COL_KB + B_W
COL_GA = COL_VB + B_W
COL_GB = COL_GA + D_MODEL
COL_KVA = COL_GB + D_MODEL
assert COL_KVA + 2 * A_KV_W == IN_WIDTH

V7X_VMEM_BYTES = 64 * 1024 * 1024
VMEM_LIMIT_BYTES = V7X_VMEM_BYTES - 8 * 1024 * 1024

PROJ_TM = 256
PROJ_TN = 512
WIN_TQ = 4 * BLK
WIN_LOOKAHEAD = 1
NBR_LOOKAHEAD = 3
MERGE_TM = 256
MLP_TM = 1024
MLP_TF = 512

BF16 = jnp.bfloat16
F32 = jnp.float32


def _rms_scale(x):
    return lax.rsqrt(jnp.mean(x * x, axis=-1, keepdims=True) + EPS)


def _issue_ahead(n, depth, first_stage, second_stage):
    pending = [first_stage(i) for i in range(min(depth, n))]
    for i in range(n):
        if i + depth < n:
            pending.append(first_stage(i + depth))
        second_stage(i, pending.pop(0))


def _head_rmsnorm(a):
    parts = []
    for c in range(a.shape[1] // HEAD_DIM):
        blk = a[:, c * HEAD_DIM:(c + 1) * HEAD_DIM]
        parts.append(blk * _rms_scale(blk))
    return jnp.concatenate(parts, axis=1)


def _in_proj_kernel(x_ref, g_ref, w_ref, cg_ref, o_ref, h_ref):
    x = x_ref[...]
    h_ref[...] = (x * _rms_scale(x) * g_ref[...]).astype(BF16)
    for c0 in range(0, IN_WIDTH, PROJ_TN):
        cols = slice(c0, c0 + PROJ_TN)
        acc = jnp.dot(h_ref[...], w_ref[:, cols], preferred_element_type=F32)
        if c0 < COL_VB:
            acc = _head_rmsnorm(acc) * cg_ref[:, cols]
        elif c0 == COL_KVA:
            k = _head_rmsnorm(acc[:, :A_KV_W]) * cg_ref[:, c0:c0 + A_KV_W]
            acc = jnp.concatenate([k, acc[:, A_KV_W:]], axis=1)
        o_ref[:, cols] = acc.astype(BF16)


def _in_proj(x2, g, w, colgain):
    m = x2.shape[0]
    resident = functools.partial(pl.BlockSpec, index_map=lambda i: (0, 0), pipeline_mode=pl.Buffered(1))
    return pl.pallas_call(
        _in_proj_kernel,
        grid=(m // PROJ_TM,),
        in_specs=[
            pl.BlockSpec((PROJ_TM, D_MODEL), lambda i: (i, 0)),
            resident((1, D_MODEL)),
            resident((D_MODEL, IN_WIDTH)),
            resident((1, IN_WIDTH)),
        ],
        out_specs=pl.BlockSpec((PROJ_TM, IN_WIDTH), lambda i: (i, 0)),
        out_shape=jax.ShapeDtypeStruct((m, IN_WIDTH), BF16),
        scratch_shapes=[pltpu.VMEM((PROJ_TM, D_MODEL), BF16)],
        compiler_params=pltpu.CompilerParams(
            dimension_semantics=("parallel",), vmem_limit_bytes=VMEM_LIMIT_BYTES),
        name="in_proj",
    )(x2, g, w, colgain)


def _win_attn_kernel(sink_ref, q_ref, kvp_ref, kvm_ref, kvn_ref, bias_ref, o_ref, *, n_tiles):
    i = pl.program_id(1)
    kv = jnp.concatenate([kvp_ref[0], kvm_ref[0], kvn_ref[0]], axis=0)
    col = lax.broadcasted_iota(jnp.int32, (1, 3 * BLK), 1)
    first_key = jnp.where(i == 0, BLK, 0)
    end_key = jnp.where(i == n_tiles - 1, 2 * BLK, 3 * BLK)
    n_blk = WIN_TQ // BLK
    units = [(t, kvh) for t in range(n_blk) for kvh in range(A_KV_HEADS)]

    def scores(t, kvh):
        k = kv[t * BLK:(t + 3) * BLK, kvh * HEAD_DIM:(kvh + 1) * HEAD_DIM]
        q = jnp.concatenate(
            [q_ref[0, t * BLK:(t + 1) * BLK, (kvh * A_GROUP + g) * HEAD_DIM:(kvh * A_GROUP + g + 1) * HEAD_DIM]
             for g in range(A_GROUP)], axis=0)
        s = lax.dot_general(q, k, (((1,), (1,)), ((), ())), preferred_element_type=F32)
        s = s + bias_ref[kvh * A_GROUP:(kvh + 1) * A_GROUP].reshape(A_GROUP * BLK, 3 * BLK)
        if t == 0:
            s = jnp.where(col >= first_key, s, NEG)
        if t == n_blk - 1:
            s = jnp.where(col < end_key, s, NEG)
        return s, jnp.max(s, axis=-1, keepdims=True)

    def finish(t, kvh, s_and_max):
        s, row_max = s_and_max
        heads = [kvh * A_GROUP + g for g in range(A_GROUP)]
        v = kv[t * BLK:(t + 3) * BLK, A_KV_W + kvh * HEAD_DIM:A_KV_W + (kvh + 1) * HEAD_DIM]
        sink = jnp.concatenate([jnp.full((BLK, 1), sink_ref[h], F32) for h in heads], axis=0)
        m = jnp.maximum(row_max, sink)
        p = jnp.exp(s - m)
        denom = jnp.sum(p, axis=-1, keepdims=True) + jnp.exp(sink - m)
        o = jnp.dot(p.astype(BF16), v, preferred_element_type=F32) / denom
        for g, h in enumerate(heads):
            o_ref[0, t * BLK:(t + 1) * BLK, h * HEAD_DIM:(h + 1) * HEAD_DIM] = (
                o[g * BLK:(g + 1) * BLK].astype(BF16))

    _issue_ahead(len(units), WIN_LOOKAHEAD, lambda u: scores(*units[u]), lambda u, s: finish(*units[u], s))


def _win_attn(proj3, sink, bias_a):
    b, seq, _ = proj3.shape
    n_tiles = seq // WIN_TQ
    blk_per_tile = WIN_TQ // BLK
    last_blk = seq // BLK - 1
    kva_wide = COL_KVA // (2 * A_KV_W)
    return pl.pallas_call(
        functools.partial(_win_attn_kernel, n_tiles=n_tiles),
        grid=(b, n_tiles),
        in_specs=[
            pl.BlockSpec(memory_space=pltpu.SMEM),
            pl.BlockSpec((1, WIN_TQ, A_Q_W), lambda bi, i: (bi, i, COL_QA // A_Q_W)),
            pl.BlockSpec((1, BLK, 2 * A_KV_W),
                         lambda bi, i: (bi, jnp.maximum(i * blk_per_tile - 1, 0), kva_wide)),
            pl.BlockSpec((1, WIN_TQ, 2 * A_KV_W), lambda bi, i: (bi, i, kva_wide)),
            pl.BlockSpec((1, BLK, 2 * A_KV_W),
                         lambda bi, i: (bi, jnp.minimum((i + 1) * blk_per_tile, last_blk), kva_wide)),
            pl.BlockSpec((A_Q_HEADS, BLK, 3 * BLK), lambda bi, i: (0, 0, 0)),
        ],
        out_specs=pl.BlockSpec((1, WIN_TQ, A_Q_W), lambda bi, i: (bi, i, 0)),
        out_shape=jax.ShapeDtypeStruct((b, seq, A_Q_W), BF16),
        compiler_params=pltpu.CompilerParams(
            dimension_semantics=("parallel", "arbitrary"), vmem_limit_bytes=VMEM_LIMIT_BYTES),
        name="win_attn",
    )(sink, proj3, proj3, proj3, proj3, bias_a)


def _nbr_attn_kernel(q_ref, k_ref, v_ref, bias_ref, o_ref, *, rows):
    def first_key_row(r):
        return min(max(r - NA_KH // 2, 0), rows - NA_KH)

    def scores(r):
        k0 = first_key_row(r) * GRID_W
        q = q_ref[0, r * GRID_W:(r + 1) * GRID_W, :]
        k = k_ref[0, k0:k0 + NA_KH * GRID_W, :]
        s = lax.dot_general(q, k, (((1,), (1,)), ((), ())), preferred_element_type=F32)
        s = s + bias_ref[0, r - k0 // GRID_W]
        return s, jnp.max(s, axis=-1, keepdims=True)

    def finish(r, s_and_max):
        s, m = s_and_max
        rs = first_key_row(r)
        v = v_ref[0, rs * GRID_W:(rs + NA_KH) * GRID_W, :]
        p = jnp.exp(s - m)
        denom = jnp.sum(p, axis=-1, keepdims=True)
        o = jnp.dot(p.astype(BF16), v, preferred_element_type=F32) / denom
        o_ref[0, r * GRID_W:(r + 1) * GRID_W, :] = o.astype(BF16)

    _issue_ahead(rows, NBR_LOOKAHEAD, scores, finish)


def _nbr_attn(proj3, bias_b):
    b, seq, _ = proj3.shape
    rows = seq // GRID_W
    return pl.pallas_call(
        functools.partial(_nbr_attn_kernel, rows=rows),
        grid=(b, B_HEADS),
        in_specs=[
            pl.BlockSpec((1, seq, HEAD_DIM), lambda bi, h: (bi, 0, COL_QB // HEAD_DIM + h)),
            pl.BlockSpec((1, seq, HEAD_DIM), lambda bi, h: (bi, 0, COL_KB // HEAD_DIM + h)),
            pl.BlockSpec((1, seq, HEAD_DIM), lambda bi, h: (bi, 0, COL_VB // HEAD_DIM + h)),
            pl.BlockSpec((1, NA_KH, GRID_W, NA_KH * GRID_W), lambda bi, h: (h, 0, 0, 0)),
        ],
        out_specs=pl.BlockSpec((1, seq, HEAD_DIM), lambda bi, h: (bi, 0, h)),
        out_shape=jax.ShapeDtypeStruct((b, seq, B_W), BF16),
        compiler_params=pltpu.CompilerParams(
            dimension_semantics=("parallel", "arbitrary"), vmem_limit_bytes=VMEM_LIMIT_BYTES),
        name="nbr_attn",
    )(proj3, proj3, proj3, bias_b)


def _merge_kernel(oa_ref, ob_ref, ga_ref, gb_ref, x_ref, wa_ref, wb_ref, wo_ref, o_ref):
    ya = jnp.dot(oa_ref[...], wa_ref[...], preferred_element_type=F32)
    yb = jnp.dot(ob_ref[...], wb_ref[...], preferred_element_type=F32)
    merged = jax.nn.sigmoid(ga_ref[...].astype(F32)) * ya + jax.nn.sigmoid(gb_ref[...].astype(F32)) * yb
    o_ref[...] = x_ref[...] + jnp.dot(merged.astype(BF16), wo_ref[...], preferred_element_type=F32)


def _merge(oa, ob, proj, x2, wa, wb, wo):
    m = x2.shape[0]
    resident = functools.partial(pl.BlockSpec, index_map=lambda i: (0, 0), pipeline_mode=pl.Buffered(1))
    return pl.pallas_call(
        _merge_kernel,
        grid=(m // MERGE_TM,),
        in_specs=[
            pl.BlockSpec((MERGE_TM, A_Q_W), lambda i: (i, 0)),
            pl.BlockSpec((MERGE_TM, B_W), lambda i: (i, 0)),
            pl.BlockSpec((MERGE_TM, D_MODEL), lambda i: (i, COL_GA // D_MODEL)),
            pl.BlockSpec((MERGE_TM, D_MODEL), lambda i: (i, COL_GB // D_MODEL)),
            pl.BlockSpec((MERGE_TM, D_MODEL), lambda i: (i, 0)),
            resident((A_Q_W, D_MODEL)),
            resident((B_W, D_MODEL)),
            resident((D_MODEL, D_MODEL)),
        ],
        out_specs=pl.BlockSpec((MERGE_TM, D_MODEL), lambda i: (i, 0)),
        out_shape=jax.ShapeDtypeStruct((m, D_MODEL), F32),
        compiler_params=pltpu.CompilerParams(
            dimension_semantics=("parallel",), vmem_limit_bytes=VMEM_LIMIT_BYTES),
        name="merge",
    )(oa, ob, proj, proj, x2, wa, wb, wo)


def _mlp_kernel(x_ref, g_ref, wu_ref, wd_ref, o_ref, h_ref):
    @pl.when(pl.program_id(1) == 0)
    def _():
        x = x_ref[...]
        h_ref[...] = (x * _rms_scale(x) * g_ref[...]).astype(BF16)
        o_ref[...] = x

    u = jnp.dot(h_ref[...], wu_ref[...], preferred_element_type=F32)
    a = jnp.square(jnp.maximum(u, 0.0)).astype(BF16)
    o_ref[...] += jnp.dot(a, wd_ref[...], preferred_element_type=F32)


def _mlp(x2, g, wu, wd):
    m = x2.shape[0]
    return pl.pallas_call(
        _mlp_kernel,
        grid=(m // MLP_TM, D_FF // MLP_TF),
        in_specs=[
            pl.BlockSpec((MLP_TM, D_MODEL), lambda i, f: (i, 0)),
            pl.BlockSpec((1, D_MODEL), lambda i, f: (0, 0)),
            pl.BlockSpec((D_MODEL, MLP_TF), lambda i, f: (0, f)),
            pl.BlockSpec((MLP_TF, D_MODEL), lambda i, f: (f, 0)),
        ],
        out_specs=pl.BlockSpec((MLP_TM, D_MODEL), lambda i, f: (i, 0)),
        out_shape=jax.ShapeDtypeStruct((m, D_MODEL), F32),
        scratch_shapes=[pltpu.VMEM((MLP_TM, D_MODEL), BF16)],
        compiler_params=pltpu.CompilerParams(
            dimension_semantics=("parallel", "arbitrary"), vmem_limit_bytes=VMEM_LIMIT_BYTES),
        name="mlp",
    )(x2, g, wu, wd)


def _t5_bucket(rel):
    nb = N_BUCKETS // 2
    max_exact = nb // 2
    ret = (rel > 0).astype(np.int32) * nb
    n = np.abs(rel).astype(np.int32)
    nf = np.maximum(n, max_exact).astype(np.float32)
    large = max_exact + (np.log(nf / max_exact) / np.log(MAX_DISTANCE / max_exact) * (nb - max_exact)).astype(np.int32)
    large = np.minimum(large, nb - 1)
    return ret + np.where(n < max_exact, n, large)


def _select_rows(onehot, table):
    return jnp.einsum("ij,j...->i...", jnp.asarray(onehot, F32), table.astype(F32), precision=lax.Precision.HIGHEST)


def _window_bias_table(t5_bias):
    n_off = 4 * BLK
    offsets = np.arange(n_off - 1) - (2 * BLK - 1)
    onehot = _t5_bucket(offsets)[:, None] == np.arange(N_BUCKETS)[None, :]
    per_offset = jnp.where((np.abs(offsets) <= WINDOW)[:, None], _select_rows(onehot, t5_bias), NEG)
    y = jnp.pad(per_offset.T, ((0, 0), (0, 1)))
    skew = jnp.tile(y, (1, BLK + 1))[:, :BLK * (n_off + 1)].reshape(A_Q_HEADS, BLK, n_off + 1)
    return skew[:, ::-1, :3 * BLK]


def _nbr_bias_table(rpb):
    qc = np.arange(GRID_W)[:, None]
    kc = np.arange(GRID_W)[None, :]
    start_c = np.clip(qc - NA_KW // 2, 0, GRID_W - NA_KW)
    col_mask = (kc >= start_c) & (kc < start_c + NA_KW)
    dcc = np.clip(kc - qc, -(NA_KW - 1), NA_KW - 1) + (NA_KW - 1)
    onehot = dcc.reshape(-1, 1) == np.arange(2 * NA_KW - 1)[None, :]
    by_col = _select_rows(onehot, jnp.transpose(rpb, (2, 0, 1)))
    by_col = jnp.where(col_mask.reshape(-1, 1, 1), by_col, NEG).reshape(GRID_W, GRID_W, B_HEADS, 2 * NA_KH - 1)
    by_col = jnp.transpose(by_col, (2, 0, 3, 1))
    variants = [by_col[:, :, NA_KH - 1 - d:2 * NA_KH - 1 - d, :] for d in range(NA_KH)]
    return jnp.stack(variants, axis=1).reshape(B_HEADS, NA_KH, GRID_W, NA_KH * GRID_W)


def _prepare(norm_mix_g, w_in, q_norm_a, k_norm_a, t5_bias, sink_a, q_norm_b, k_norm_b, rpb_b,
             w_br_a, w_br_b, w_out, norm_mlp_g, w_up, w_down):
    w = w_in[0]
    o_qa, o_ka, o_va, o_qb, o_kb, o_vb, o_ga, o_gb = np.cumsum(
        (0, A_Q_W, A_KV_W, A_KV_W, B_W, B_W, B_W, D_MODEL))
    w_perm = jnp.concatenate([
        w[:, o_qa:o_ka], w[:, o_qb:o_kb], w[:, o_kb:o_vb], w[:, o_vb:o_ga], w[:, o_ga:o_gb], w[:, o_gb:],
        w[:, o_ka:o_va], w[:, o_va:o_qb]], axis=1).astype(BF16)
    scale = HEAD_DIM ** -0.5
    ones = jnp.ones((IN_WIDTH - COL_VB - 2 * A_KV_W,), F32)
    colgain = jnp.concatenate([
        jnp.tile(q_norm_a[0].astype(F32), A_Q_HEADS) * scale,
        jnp.tile(q_norm_b[0].astype(F32), B_HEADS) * scale,
        jnp.tile(k_norm_b[0].astype(F32), B_HEADS),
        ones,
        jnp.tile(k_norm_a[0].astype(F32), A_KV_HEADS),
        jnp.ones((A_KV_W,), F32)]).reshape(1, IN_WIDTH)
    return dict(
        g_mix=norm_mix_g[0].astype(F32).reshape(1, D_MODEL),
        w_in=w_perm,
        colgain=colgain,
        sink=sink_a[0].astype(F32),
        bias_a=_window_bias_table(t5_bias),
        bias_b=_nbr_bias_table(rpb_b[0]),
        w_a=w_br_a[0].astype(BF16),
        w_b=w_br_b[0].astype(BF16),
        w_o=w_out[0].astype(BF16),
        g_mlp=norm_mlp_g[0].astype(F32).reshape(1, D_MODEL),
        w_up=w_up[0].astype(BF16),
        w_down=w_down[0].astype(BF16),
    )


def _layer(x, p):
    b, seq, _ = x.shape
    x2 = x.reshape(b * seq, D_MODEL)
    proj = _in_proj(x2, p["g_mix"], p["w_in"], p["colgain"])
    proj3 = proj.reshape(b, seq, IN_WIDTH)
    oa = _win_attn(proj3, p["sink"], p["bias_a"]).reshape(b * seq, A_Q_W)
    ob = _nbr_attn(proj3, p["bias_b"]).reshape(b * seq, B_W)
    x1 = _merge(oa, ob, proj, x2, p["w_a"], p["w_b"], p["w_o"])
    y = _mlp(x1, p["g_mlp"], p["w_up"], p["w_down"])
    return y.reshape(b, seq, D_MODEL)


def kernel(x_prompt, x_sample, norm_mix_g, w_in, q_norm_a, k_norm_a, t5_bias, sink_a, q_norm_b, k_norm_b, rpb_b,
           w_br_a, w_br_b, w_out, norm_mlp_g, w_up, w_down):
    p = _prepare(norm_mix_g, w_in, q_norm_a, k_norm_a, t5_bias, sink_a, q_norm_b, k_norm_b, rpb_b,
                 w_br_a, w_br_b, w_out, norm_mlp_g, w_up, w_down)
    return (_layer(x_prompt, p), _layer(x_sample, p))
```

```python
import functools

import jax
import jax.numpy as jnp
import numpy as np
from jax import lax
from jax.experimental import pallas as pl
from jax.experimental.pallas import tpu as pltpu

D_MODEL = 2048
HEAD_DIM = 128
A_Q_HEADS = 8
A_KV_HEADS = 2
A_GROUP = A_Q_HEADS // A_KV_HEADS
WINDOW = 128
BLK = WINDOW
N_BUCKETS = 32
MAX_DISTANCE = 128
B_HEADS = 8
GRID_W = 64
NA_KH = 8
NA_KW = 16
D_FF = 4 * D_MODEL
EPS = 1e-6
NEG = -1e30

A_Q_W = A_Q_HEADS * HEAD_DIM
A_KV_W = A_KV_HEADS * HEAD_DIM
B_W = B_HEADS * HEAD_DIM
IN_WIDTH = A_Q_W + 2 * A_KV_W + 3 * B_W + 2 * D_MODEL

COL_GA = 0
COL_GB = COL_GA + D_MODEL
COL_QA = COL_GB + D_MODEL
COL_KVA = COL_QA + A_Q_W
PROJ_W = COL_KVA + 2 * A_KV_W
COL_QB = PROJ_W
COL_KB = COL_QB + B_W
COL_VB = COL_KB + B_W
assert COL_VB + B_W == IN_WIDTH

V7X_VMEM_BYTES = 64 * 1024 * 1024
VMEM_LIMIT_BYTES = V7X_VMEM_BYTES - 8 * 1024 * 1024

PROJ_TM = 256
PROJ_TN = 512
WIN_TQ = 4 * BLK
WIN_LOOKAHEAD = 2
NBR_LOOKAHEAD = 3
MERGE_TM = 256
MLP_TM = 1024
MLP_TF = 512

BF16 = jnp.bfloat16
F32 = jnp.float32


def _rms_scale(x):
    return lax.rsqrt(jnp.mean(x * x, axis=-1, keepdims=True) + EPS)


def _issue_ahead(n, depth, first_stage, second_stage):
    pending = [first_stage(i) for i in range(min(depth, n))]
    for i in range(n):
        if i + depth < n:
            pending.append(first_stage(i + depth))
        second_stage(i, pending.pop(0))


def _head_rmsnorm(a):
    parts = []
    for c in range(a.shape[1] // HEAD_DIM):
        blk = a[:, c * HEAD_DIM:(c + 1) * HEAD_DIM]
        parts.append(blk * _rms_scale(blk))
    return jnp.concatenate(parts, axis=1)


def _in_proj_kernel(x_ref, g_ref, w_ref, cg_ref, o_ref, hb_ref, h_ref):
    x = x_ref[...]
    h_ref[...] = (x * _rms_scale(x) * g_ref[...]).astype(BF16)
    for c0 in range(0, IN_WIDTH, PROJ_TN):
        cols = slice(c0, c0 + PROJ_TN)
        acc = jnp.dot(h_ref[...], w_ref[:, cols], preferred_element_type=F32)
        if COL_QA <= c0 < COL_KVA or COL_QB <= c0 < COL_VB:
            acc = _head_rmsnorm(acc) * cg_ref[:, cols]
        elif c0 == COL_KVA:
            k = _head_rmsnorm(acc[:, :A_KV_W]) * cg_ref[:, c0:c0 + A_KV_W]
            acc = jnp.concatenate([k, acc[:, A_KV_W:]], axis=1)
        if c0 < PROJ_W:
            o_ref[:, cols] = acc.astype(BF16)
        else:
            for j in range(PROJ_TN // HEAD_DIM):
                hb_ref[(c0 - PROJ_W) // HEAD_DIM + j] = acc[:, j * HEAD_DIM:(j + 1) * HEAD_DIM].astype(BF16)


def _in_proj(x2, g, w, colgain):
    m = x2.shape[0]
    resident = functools.partial(pl.BlockSpec, index_map=lambda i: (0, 0), pipeline_mode=pl.Buffered(1))
    return pl.pallas_call(
        _in_proj_kernel,
        grid=(m // PROJ_TM,),
        in_specs=[
            pl.BlockSpec((PROJ_TM, D_MODEL), lambda i: (i, 0)),
            resident((1, D_MODEL)),
            resident((D_MODEL, IN_WIDTH)),
            resident((1, IN_WIDTH)),
        ],
        out_specs=[
            pl.BlockSpec((PROJ_TM, PROJ_W), lambda i: (i, 0)),
            pl.BlockSpec((3 * B_HEADS, PROJ_TM, HEAD_DIM), lambda i: (0, i, 0)),
        ],
        out_shape=[
            jax.ShapeDtypeStruct((m, PROJ_W), BF16),
            jax.ShapeDtypeStruct((3 * B_HEADS, m, HEAD_DIM), BF16),
        ],
        scratch_shapes=[pltpu.VMEM((PROJ_TM, D_MODEL), BF16)],
        compiler_params=pltpu.CompilerParams(
            dimension_semantics=("parallel",), vmem_limit_bytes=VMEM_LIMIT_BYTES),
        name="in_proj",
    )(x2, g, w, colgain)


def _win_attn_kernel(sink_ref, q_ref, kvp_ref, kvm_ref, kvn_ref, bias_ref, o_ref, *, n_tiles):
    i = pl.program_id(1)
    kv = jnp.concatenate([kvp_ref[0], kvm_ref[0], kvn_ref[0]], axis=0)
    key = lax.broadcasted_iota(jnp.int32, (3 * BLK, 1), 0)
    first_key = jnp.where(i == 0, BLK, 0)
    end_key = jnp.where(i == n_tiles - 1, 2 * BLK, 3 * BLK)
    n_blk = WIN_TQ // BLK
    units = [(t, kvh) for t in range(n_blk) for kvh in range(A_KV_HEADS)]

    def scores(t, kvh):
        k = kv[t * BLK:(t + 3) * BLK, kvh * HEAD_DIM:(kvh + 1) * HEAD_DIM]
        q = jnp.concatenate(
            [q_ref[0, t * BLK:(t + 1) * BLK, (kvh * A_GROUP + g) * HEAD_DIM:(kvh * A_GROUP + g + 1) * HEAD_DIM]
             for g in range(A_GROUP)], axis=0)
        s = lax.dot_general(k, q, (((1,), (1,)), ((), ())), preferred_element_type=F32)
        s = s + bias_ref[kvh]
        if t == 0:
            s = jnp.where(key >= first_key, s, NEG)
        if t == n_blk - 1:
            s = jnp.where(key < end_key, s, NEG)
        return s, jnp.max(s, axis=0, keepdims=True)

    def finish(t, kvh, s_and_max):
        s, col_max = s_and_max
        heads = [kvh * A_GROUP + g for g in range(A_GROUP)]
        v = kv[t * BLK:(t + 3) * BLK, A_KV_W + kvh * HEAD_DIM:A_KV_W + (kvh + 1) * HEAD_DIM]
        sink = jnp.concatenate([jnp.full((1, BLK), sink_ref[h], F32) for h in heads], axis=1)
        m = jnp.maximum(col_max, sink)
        p = jnp.exp(s - m)
        denom = jnp.sum(p, axis=0, keepdims=True) + jnp.exp(sink - m)
        o = lax.dot_general(v, p.astype(BF16), (((0,), (0,)), ((), ())), preferred_element_type=F32) / denom
        for g, h in enumerate(heads):
            o_ref[0, h * HEAD_DIM:(h + 1) * HEAD_DIM, t * BLK:(t + 1) * BLK] = (
                o[:, g * BLK:(g + 1) * BLK].astype(BF16))

    _issue_ahead(len(units), WIN_LOOKAHEAD, lambda u: scores(*units[u]), lambda u, s: finish(*units[u], s))


def _win_attn(proj3, sink, bias_a):
    b, seq, _ = proj3.shape
    n_tiles = seq // WIN_TQ
    blk_per_tile = WIN_TQ // BLK
    last_blk = seq // BLK - 1
    kva_wide = COL_KVA // (2 * A_KV_W)
    return pl.pallas_call(
        functools.partial(_win_attn_kernel, n_tiles=n_tiles),
        grid=(b, n_tiles),
        in_specs=[
            pl.BlockSpec(memory_space=pltpu.SMEM),
            pl.BlockSpec((1, WIN_TQ, A_Q_W), lambda bi, i: (bi, i, COL_QA // A_Q_W)),
            pl.BlockSpec((1, BLK, 2 * A_KV_W),
                         lambda bi, i: (bi, jnp.maximum(i * blk_per_tile - 1, 0), kva_wide)),
            pl.BlockSpec((1, WIN_TQ, 2 * A_KV_W), lambda bi, i: (bi, i, kva_wide)),
            pl.BlockSpec((1, BLK, 2 * A_KV_W),
                         lambda bi, i: (bi, jnp.minimum((i + 1) * blk_per_tile, last_blk), kva_wide)),
            pl.BlockSpec((A_KV_HEADS, 3 * BLK, A_GROUP * BLK), lambda bi, i: (0, 0, 0)),
        ],
        out_specs=pl.BlockSpec((1, A_Q_W, WIN_TQ), lambda bi, i: (bi, 0, i)),
        out_shape=jax.ShapeDtypeStruct((b, A_Q_W, seq), BF16),
        compiler_params=pltpu.CompilerParams(
            dimension_semantics=("parallel", "arbitrary"), vmem_limit_bytes=VMEM_LIMIT_BYTES),
        name="win_attn",
    )(sink, proj3, proj3, proj3, proj3, bias_a)


def _nbr_attn_kernel(q_ref, k_ref, v_ref, bias_ref, o_ref, *, rows):
    def first_key_row(r):
        return min(max(r - NA_KH // 2, 0), rows - NA_KH)

    def scores(r):
        k0 = first_key_row(r) * GRID_W
        q = q_ref[0, r * GRID_W:(r + 1) * GRID_W, :]
        k = k_ref[0, k0:k0 + NA_KH * GRID_W, :]
        s = lax.dot_general(q, k, (((1,), (1,)), ((), ())), preferred_element_type=F32)
        s = s + bias_ref[0, r - k0 // GRID_W]
        return s, jnp.max(s, axis=-1, keepdims=True)

    def finish(r, s_and_max):
        s, m = s_and_max
        rs = first_key_row(r)
        v = v_ref[0, rs * GRID_W:(rs + NA_KH) * GRID_W, :]
        p = jnp.exp(s - m)
        denom = jnp.sum(p, axis=-1, keepdims=True)
        o = jnp.dot(p.astype(BF16), v, preferred_element_type=F32) / denom
        o_ref[0, r * GRID_W:(r + 1) * GRID_W, :] = o.astype(BF16)

    _issue_ahead(rows, NBR_LOOKAHEAD, scores, finish)


def _nbr_attn(heads_b, b, seq, bias_b):
    rows = seq // GRID_W
    return pl.pallas_call(
        functools.partial(_nbr_attn_kernel, rows=rows),
        grid=(b, B_HEADS),
        in_specs=[
            pl.BlockSpec((1, seq, HEAD_DIM), lambda bi, h: (h, bi, 0)),
            pl.BlockSpec((1, seq, HEAD_DIM), lambda bi, h: (B_HEADS + h, bi, 0)),
            pl.BlockSpec((1, seq, HEAD_DIM), lambda bi, h: (2 * B_HEADS + h, bi, 0)),
            pl.BlockSpec((1, NA_KH, GRID_W, NA_KH * GRID_W), lambda bi, h: (h, 0, 0, 0)),
        ],
        out_specs=pl.BlockSpec((1, seq, HEAD_DIM), lambda bi, h: (h, bi, 0)),
        out_shape=jax.ShapeDtypeStruct((B_HEADS, b * seq, HEAD_DIM), BF16),
        compiler_params=pltpu.CompilerParams(
            dimension_semantics=("parallel", "arbitrary"), vmem_limit_bytes=VMEM_LIMIT_BYTES),
        name="nbr_attn",
    )(heads_b, heads_b, heads_b, bias_b)


def _merge_kernel(oat_ref, ob_ref, ga_ref, gb_ref, x_ref, wa_ref, wb_ref, wo_ref, o_ref):
    ya = lax.dot_general(oat_ref[0], wa_ref[...], (((0,), (0,)), ((), ())), preferred_element_type=F32)
    ob = jnp.concatenate([ob_ref[h] for h in range(B_HEADS)], axis=1)
    yb = jnp.dot(ob, wb_ref[...], preferred_element_type=F32)
    merged = jax.nn.sigmoid(ga_ref[...].astype(F32)) * ya + jax.nn.sigmoid(gb_ref[...].astype(F32)) * yb
    o_ref[...] = x_ref[...] + jnp.dot(merged.astype(BF16), wo_ref[...], preferred_element_type=F32)


def _merge(oat, ob, proj, x2, wa, wb, wo):
    m = x2.shape[0]
    tiles_per_seq = oat.shape[2] // MERGE_TM
    resident = functools.partial(pl.BlockSpec, index_map=lambda i: (0, 0), pipeline_mode=pl.Buffered(1))
    return pl.pallas_call(
        _merge_kernel,
        grid=(m // MERGE_TM,),
        in_specs=[
            pl.BlockSpec((1, A_Q_W, MERGE_TM), lambda i: (i // tiles_per_seq, 0, i % tiles_per_seq)),
            pl.BlockSpec((B_HEADS, MERGE_TM, HEAD_DIM), lambda i: (0, i, 0)),
            pl.BlockSpec((MERGE_TM, D_MODEL), lambda i: (i, COL_GA // D_MODEL)),
            pl.BlockSpec((MERGE_TM, D_MODEL), lambda i: (i, COL_GB // D_MODEL)),
            pl.BlockSpec((MERGE_TM, D_MODEL), lambda i: (i, 0)),
            resident((A_Q_W, D_MODEL)),
            resident((B_W, D_MODEL)),
            resident((D_MODEL, D_MODEL)),
        ],
        out_specs=pl.BlockSpec((MERGE_TM, D_MODEL), lambda i: (i, 0)),
        out_shape=jax.ShapeDtypeStruct((m, D_MODEL), F32),
        compiler_params=pltpu.CompilerParams(
            dimension_semantics=("parallel",), vmem_limit_bytes=VMEM_LIMIT_BYTES),
        name="merge",
    )(oat, ob, proj, proj, x2, wa, wb, wo)


def _mlp_kernel(x_ref, g_ref, wu_ref, wd_ref, o_ref, h_ref):
    @pl.when(pl.program_id(1) == 0)
    def _():
        x = x_ref[...]
        h_ref[...] = (x * _rms_scale(x) * g_ref[...]).astype(BF16)
        o_ref[...] = x

    u = jnp.dot(h_ref[...], wu_ref[...], preferred_element_type=F32)
    a = jnp.square(jnp.maximum(u, 0.0)).astype(BF16)
    o_ref[...] += jnp.dot(a, wd_ref[...], preferred_element_type=F32)


def _mlp(x2, g, wu, wd):
    m = x2.shape[0]
    return pl.pallas_call(
        _mlp_kernel,
        grid=(m // MLP_TM, D_FF // MLP_TF),
        in_specs=[
            pl.BlockSpec((MLP_TM, D_MODEL), lambda i, f: (i, 0)),
            pl.BlockSpec((1, D_MODEL), lambda i, f: (0, 0)),
            pl.BlockSpec((D_MODEL, MLP_TF), lambda i, f: (0, f)),
            pl.BlockSpec((MLP_TF, D_MODEL), lambda i, f: (f, 0)),
        ],
        out_specs=pl.BlockSpec((MLP_TM, D_MODEL), lambda i, f: (i, 0)),
        out_shape=jax.ShapeDtypeStruct((m, D_MODEL), F32),
        scratch_shapes=[pltpu.VMEM((MLP_TM, D_MODEL), BF16)],
        compiler_params=pltpu.CompilerParams(
            dimension_semantics=("parallel", "arbitrary"), vmem_limit_bytes=VMEM_LIMIT_BYTES),
        name="mlp",
    )(x2, g, wu, wd)


def _t5_bucket(rel):
    nb = N_BUCKETS // 2
    max_exact = nb // 2
    ret = (rel > 0).astype(np.int32) * nb
    n = np.abs(rel).astype(np.int32)
    nf = np.maximum(n, max_exact).astype(np.float32)
    large = max_exact + (np.log(nf / max_exact) / np.log(MAX_DISTANCE / max_exact) * (nb - max_exact)).astype(np.int32)
    large = np.minimum(large, nb - 1)
    return ret + np.where(n < max_exact, n, large)


def _select_rows(onehot, table):
    return jnp.einsum("ij,j...->i...", jnp.asarray(onehot, F32), table.astype(F32), precision=lax.Precision.HIGHEST)


def _window_bias_table(t5_bias):
    n_off = 4 * BLK
    offsets = np.arange(n_off - 1) - (2 * BLK - 1)
    onehot = _t5_bucket(offsets)[:, None] == np.arange(N_BUCKETS)[None, :]
    per_offset = jnp.where((np.abs(offsets) <= WINDOW)[:, None], _select_rows(onehot, t5_bias), NEG)
    y = jnp.pad(per_offset.T, ((0, 0), (0, 1)))
    skew = jnp.tile(y, (1, BLK + 1))[:, :BLK * (n_off + 1)].reshape(A_Q_HEADS, BLK, n_off + 1)
    table = skew[:, ::-1, :3 * BLK].reshape(A_KV_HEADS, A_GROUP, BLK, 3 * BLK)
    return jnp.transpose(table, (0, 3, 1, 2)).reshape(A_KV_HEADS, 3 * BLK, A_GROUP * BLK)


def _nbr_bias_table(rpb):
    qc = np.arange(GRID_W)[:, None]
    kc = np.arange(GRID_W)[None, :]
    start_c = np.clip(qc - NA_KW // 2, 0, GRID_W - NA_KW)
    col_mask = (kc >= start_c) & (kc < start_c + NA_KW)
    dcc = np.clip(kc - qc, -(NA_KW - 1), NA_KW - 1) + (NA_KW - 1)
    onehot = dcc.reshape(-1, 1) == np.arange(2 * NA_KW - 1)[None, :]
    by_col = _select_rows(onehot, jnp.transpose(rpb, (2, 0, 1)))
    by_col = jnp.where(col_mask.reshape(-1, 1, 1), by_col, NEG).reshape(GRID_W, GRID_W, B_HEADS, 2 * NA_KH - 1)
    by_col = jnp.transpose(by_col, (2, 0, 3, 1))
    variants = [by_col[:, :, NA_KH - 1 - d:2 * NA_KH - 1 - d, :] for d in range(NA_KH)]
    return jnp.stack(variants, axis=1).reshape(B_HEADS, NA_KH, GRID_W, NA_KH * GRID_W)


def _prepare(norm_mix_g, w_in, q_norm_a, k_norm_a, t5_bias, sink_a, q_norm_b, k_norm_b, rpb_b,
             w_br_a, w_br_b, w_out, norm_mlp_g, w_up, w_down):
    w = w_in[0]
    o_qa, o_ka, o_va, o_qb, o_kb, o_vb, o_ga, o_gb = np.cumsum(
        (0, A_Q_W, A_KV_W, A_KV_W, B_W, B_W, B_W, D_MODEL))
    w_perm = jnp.concatenate([
        w[:, o_ga:o_gb], w[:, o_gb:], w[:, o_qa:o_ka], w[:, o_ka:o_va], w[:, o_va:o_qb],
        w[:, o_qb:o_kb], w[:, o_kb:o_vb], w[:, o_vb:o_ga]], axis=1).astype(BF16)
    scale = HEAD_DIM ** -0.5
    colgain = jnp.concatenate([
        jnp.ones((2 * D_MODEL,), F32),
        jnp.tile(q_norm_a[0].astype(F32), A_Q_HEADS) * scale,
        jnp.tile(k_norm_a[0].astype(F32), A_KV_HEADS),
        jnp.ones((A_KV_W,), F32),
        jnp.tile(q_norm_b[0].astype(F32), B_HEADS) * scale,
        jnp.tile(k_norm_b[0].astype(F32), B_HEADS),
        jnp.ones((B_W,), F32)]).reshape(1, IN_WIDTH)
    return dict(
        g_mix=norm_mix_g[0].astype(F32).reshape(1, D_MODEL),
        w_in=w_perm,
        colgain=colgain,
        sink=sink_a[0].astype(F32),
        bias_a=_window_bias_table(t5_bias),
        bias_b=_nbr_bias_table(rpb_b[0]),
        w_a=w_br_a[0].astype(BF16),
        w_b=w_br_b[0].astype(BF16),
        w_o=w_out[0].astype(BF16),
        g_mlp=norm_mlp_g[0].astype(F32).reshape(1, D_MODEL),
        w_up=w_up[0].astype(BF16),
        w_down=w_down[0].astype(BF16),
    )


def _layer(x, p):
    b, seq, _ = x.shape
    x2 = x.reshape(b * seq, D_MODEL)
    proj, heads_b = _in_proj(x2, p["g_mix"], p["w_in"], p["colgain"])
    proj3 = proj.reshape(b, seq, PROJ_W)
    oat = _win_attn(proj3, p["sink"], p["bias_a"])
    ob = _nbr_attn(heads_b, b, seq, p["bias_b"])
    x1 = _merge(oat, ob, proj, x2, p["w_a"], p["w_b"], p["w_o"])
    y = _mlp(x1, p["g_mlp"], p["w_up"], p["w_down"])
    return y.reshape(b, seq, D_MODEL)


def kernel(x_prompt, x_sample, norm_mix_g, w_in, q_norm_a, k_norm_a, t5_bias, sink_a, q_norm_b, k_norm_b, rpb_b,
           w_br_a, w_br_b, w_out, norm_mlp_g, w_up, w_down):
    p = _prepare(norm_mix_g, w_in, q_norm_a, k_norm_a, t5_bias, sink_a, q_norm_b, k_norm_b, rpb_b,
                 w_br_a, w_br_b, w_out, norm_mlp_g, w_up, w_down)
    return (_layer(x_prompt, p), _layer(x_sample, p))
```

```python
import functools

import jax
import jax.numpy as jnp
import numpy as np
from jax import lax
from jax.experimental import pallas as pl
from jax.experimental.pallas import tpu as pltpu

D_MODEL = 2048
HEAD_DIM = 128
A_Q_HEADS = 8
A_KV_HEADS = 2
A_GROUP = A_Q_HEADS // A_KV_HEADS
WINDOW = 128
BLK = WINDOW
N_BUCKETS = 32
MAX_DISTANCE = 128
B_HEADS = 8
GRID_W = 64
NA_KH = 8
NA_KW = 16
D_FF = 4 * D_MODEL
EPS = 1e-6
NEG = -1e30

A_Q_W = A_Q_HEADS * HEAD_DIM
A_KV_W = A_KV_HEADS * HEAD_DIM
B_W = B_HEADS * HEAD_DIM
IN_WIDTH = A_Q_W + 2 * A_KV_W + 3 * B_W + 2 * D_MODEL

COL_GA = 0
COL_GB = COL_GA + D_MODEL
COL_QA = COL_GB + D_MODEL
COL_KVA = COL_QA + A_Q_W
PROJ_W = COL_KVA + 2 * A_KV_W
COL_QB = PROJ_W
COL_KB = COL_QB + B_W
COL_VB = COL_KB + B_W
assert COL_VB + B_W == IN_WIDTH

V7X_VMEM_BYTES = 64 * 1024 * 1024
VMEM_LIMIT_BYTES = V7X_VMEM_BYTES - 8 * 1024 * 1024

PROJ_TM = 256
PROJ_TN = 512
WIN_TQ = 4 * BLK
WIN_LOOKAHEAD = 2
NBR_LOOKAHEAD = 3
MERGE_TM = 512
MERGE_SUB = 256
MLP_TM = 1024
MLP_TF = 512

BF16 = jnp.bfloat16
F32 = jnp.float32


def _rms_scale(x):
    return lax.rsqrt(jnp.mean(x * x, axis=-1, keepdims=True) + EPS)


def _issue_ahead(n, depth, first_stage, second_stage):
    pending = [first_stage(i) for i in range(min(depth, n))]
    for i in range(n):
        if i + depth < n:
            pending.append(first_stage(i + depth))
        second_stage(i, pending.pop(0))


def _head_rmsnorm(a):
    parts = []
    for c in range(a.shape[1] // HEAD_DIM):
        blk = a[:, c * HEAD_DIM:(c + 1) * HEAD_DIM]
        parts.append(blk * _rms_scale(blk))
    return jnp.concatenate(parts, axis=1)


def _in_proj_kernel(x_ref, g_ref, w_ref, cg_ref, o_ref, hb_ref, h_ref):
    x = x_ref[...]
    h_ref[...] = (x * _rms_scale(x) * g_ref[...]).astype(BF16)
    for c0 in range(0, IN_WIDTH, PROJ_TN):
        cols = slice(c0, c0 + PROJ_TN)
        acc = jnp.dot(h_ref[...], w_ref[:, cols], preferred_element_type=F32)
        if COL_QA <= c0 < COL_KVA or COL_QB <= c0 < COL_VB:
            acc = _head_rmsnorm(acc) * cg_ref[:, cols]
        elif c0 == COL_KVA:
            k = _head_rmsnorm(acc[:, :A_KV_W]) * cg_ref[:, c0:c0 + A_KV_W]
            acc = jnp.concatenate([k, acc[:, A_KV_W:]], axis=1)
        if c0 < PROJ_W:
            o_ref[:, cols] = acc.astype(BF16)
        else:
            for j in range(PROJ_TN // HEAD_DIM):
                hb_ref[(c0 - PROJ_W) // HEAD_DIM + j] = acc[:, j * HEAD_DIM:(j + 1) * HEAD_DIM].astype(BF16)


def _in_proj(x2, g, w, colgain):
    m = x2.shape[0]
    resident = functools.partial(pl.BlockSpec, index_map=lambda i: (0, 0), pipeline_mode=pl.Buffered(1))
    return pl.pallas_call(
        _in_proj_kernel,
        grid=(m // PROJ_TM,),
        in_specs=[
            pl.BlockSpec((PROJ_TM, D_MODEL), lambda i: (i, 0)),
            resident((1, D_MODEL)),
            resident((D_MODEL, IN_WIDTH)),
            resident((1, IN_WIDTH)),
        ],
        out_specs=[
            pl.BlockSpec((PROJ_TM, PROJ_W), lambda i: (i, 0)),
            pl.BlockSpec((3 * B_HEADS, PROJ_TM, HEAD_DIM), lambda i: (0, i, 0)),
        ],
        out_shape=[
            jax.ShapeDtypeStruct((m, PROJ_W), BF16),
            jax.ShapeDtypeStruct((3 * B_HEADS, m, HEAD_DIM), BF16),
        ],
        scratch_shapes=[pltpu.VMEM((PROJ_TM, D_MODEL), BF16)],
        compiler_params=pltpu.CompilerParams(
            dimension_semantics=("parallel",), vmem_limit_bytes=VMEM_LIMIT_BYTES),
        name="in_proj",
    )(x2, g, w, colgain)


def _win_attn_kernel(sink_ref, q_ref, kvp_ref, kvm_ref, kvn_ref, bias_ref, o_ref, *, n_tiles):
    i = pl.program_id(1)
    kv = jnp.concatenate([kvp_ref[0], kvm_ref[0], kvn_ref[0]], axis=0)
    key = lax.broadcasted_iota(jnp.int32, (3 * BLK, 1), 0)
    first_key = jnp.where(i == 0, BLK, 0)
    end_key = jnp.where(i == n_tiles - 1, 2 * BLK, 3 * BLK)
    n_blk = WIN_TQ // BLK
    units = [(t, kvh) for t in range(n_blk) for kvh in range(A_KV_HEADS)]

    def scores(t, kvh):
        k = kv[t * BLK:(t + 3) * BLK, kvh * HEAD_DIM:(kvh + 1) * HEAD_DIM]
        q = jnp.concatenate(
            [q_ref[0, t * BLK:(t + 1) * BLK, (kvh * A_GROUP + g) * HEAD_DIM:(kvh * A_GROUP + g + 1) * HEAD_DIM]
             for g in range(A_GROUP)], axis=0)
        s = lax.dot_general(k, q, (((1,), (1,)), ((), ())), preferred_element_type=F32)
        s = s + bias_ref[kvh]
        if t == 0:
            s = jnp.where(key >= first_key, s, NEG)
        if t == n_blk - 1:
            s = jnp.where(key < end_key, s, NEG)
        return s, jnp.max(s, axis=0, keepdims=True)

    def finish(t, kvh, s_and_max):
        s, col_max = s_and_max
        heads = [kvh * A_GROUP + g for g in range(A_GROUP)]
        v = kv[t * BLK:(t + 3) * BLK, A_KV_W + kvh * HEAD_DIM:A_KV_W + (kvh + 1) * HEAD_DIM]
        sink = jnp.concatenate([jnp.full((1, BLK), sink_ref[h], F32) for h in heads], axis=1)
        m = jnp.maximum(col_max, sink)
        p = jnp.exp(s - m)
        denom = jnp.sum(p, axis=0, keepdims=True) + jnp.exp(sink - m)
        o = lax.dot_general(v, p.astype(BF16), (((0,), (0,)), ((), ())), preferred_element_type=F32) / denom
        for g, h in enumerate(heads):
            o_ref[0, h * HEAD_DIM:(h + 1) * HEAD_DIM, t * BLK:(t + 1) * BLK] = (
                o[:, g * BLK:(g + 1) * BLK].astype(BF16))

    _issue_ahead(len(units), WIN_LOOKAHEAD, lambda u: scores(*units[u]), lambda u, s: finish(*units[u], s))


def _win_attn(proj3, sink, bias_a):
    b, seq, _ = proj3.shape
    n_tiles = seq // WIN_TQ
    blk_per_tile = WIN_TQ // BLK
    last_blk = seq // BLK - 1
    kva_wide = COL_KVA // (2 * A_KV_W)
    return pl.pallas_call(
        functools.partial(_win_attn_kernel, n_tiles=n_tiles),
        grid=(b, n_tiles),
        in_specs=[
            pl.BlockSpec(memory_space=pltpu.SMEM),
            pl.BlockSpec((1, WIN_TQ, A_Q_W), lambda bi, i: (bi, i, COL_QA // A_Q_W)),
            pl.BlockSpec((1, BLK, 2 * A_KV_W),
                         lambda bi, i: (bi, jnp.maximum(i * blk_per_tile - 1, 0), kva_wide)),
            pl.BlockSpec((1, WIN_TQ, 2 * A_KV_W), lambda bi, i: (bi, i, kva_wide)),
            pl.BlockSpec((1, BLK, 2 * A_KV_W),
                         lambda bi, i: (bi, jnp.minimum((i + 1) * blk_per_tile, last_blk), kva_wide)),
            pl.BlockSpec((A_KV_HEADS, 3 * BLK, A_GROUP * BLK), lambda bi, i: (0, 0, 0)),
        ],
        out_specs=pl.BlockSpec((1, A_Q_W, WIN_TQ), lambda bi, i: (bi, 0, i)),
        out_shape=jax.ShapeDtypeStruct((b, A_Q_W, seq), BF16),
        compiler_params=pltpu.CompilerParams(
            dimension_semantics=("parallel", "arbitrary"), vmem_limit_bytes=VMEM_LIMIT_BYTES),
        name="win_attn",
    )(sink, proj3, proj3, proj3, proj3, bias_a)


def _nbr_attn_kernel(q_ref, k_ref, v_ref, bias_ref, o_ref, *, rows):
    def first_key_row(r):
        return min(max(r - NA_KH // 2, 0), rows - NA_KH)

    def scores(r):
        k0 = first_key_row(r) * GRID_W
        q = q_ref[0, r * GRID_W:(r + 1) * GRID_W, :]
        k = k_ref[0, k0:k0 + NA_KH * GRID_W, :]
        s = lax.dot_general(q, k, (((1,), (1,)), ((), ())), preferred_element_type=F32)
        s = s + bias_ref[0, r - k0 // GRID_W]
        return s, jnp.max(s, axis=-1, keepdims=True)

    def finish(r, s_and_max):
        s, m = s_and_max
        rs = first_key_row(r)
        v = v_ref[0, rs * GRID_W:(rs + NA_KH) * GRID_W, :]
        p = jnp.exp(s - m)
        denom = jnp.sum(p, axis=-1, keepdims=True)
        o = jnp.dot(p.astype(BF16), v, preferred_element_type=F32) / denom
        o_ref[0, r * GRID_W:(r + 1) * GRID_W, :] = o.astype(BF16)

    _issue_ahead(rows, NBR_LOOKAHEAD, scores, finish)


def _nbr_attn(heads_b, b, seq, bias_b):
    rows = seq // GRID_W
    return pl.pallas_call(
        functools.partial(_nbr_attn_kernel, rows=rows),
        grid=(b, B_HEADS),
        in_specs=[
            pl.BlockSpec((1, seq, HEAD_DIM), lambda bi, h: (h, bi, 0)),
            pl.BlockSpec((1, seq, HEAD_DIM), lambda bi, h: (B_HEADS + h, bi, 0)),
            pl.BlockSpec((1, seq, HEAD_DIM), lambda bi, h: (2 * B_HEADS + h, bi, 0)),
            pl.BlockSpec((1, NA_KH, GRID_W, NA_KH * GRID_W), lambda bi, h: (h, 0, 0, 0)),
        ],
        out_specs=pl.BlockSpec((1, seq, HEAD_DIM), lambda bi, h: (h, bi, 0)),
        out_shape=jax.ShapeDtypeStruct((B_HEADS, b * seq, HEAD_DIM), BF16),
        compiler_params=pltpu.CompilerParams(
            dimension_semantics=("parallel", "arbitrary"), vmem_limit_bytes=VMEM_LIMIT_BYTES),
        name="nbr_attn",
    )(heads_b, heads_b, heads_b, bias_b)


def _merge_kernel(oat_ref, ob_ref, ga_ref, gb_ref, x_ref, wa_ref, wb_ref, wo_ref, o_ref):
    merged = []
    for r0 in range(0, MERGE_TM, MERGE_SUB):
        rows = slice(r0, r0 + MERGE_SUB)
        ya = lax.dot_general(oat_ref[0, :, rows], wa_ref[...], (((0,), (0,)), ((), ())), preferred_element_type=F32)
        ob = jnp.concatenate([ob_ref[h, rows, :] for h in range(B_HEADS)], axis=1)
        yb = jnp.dot(ob, wb_ref[...], preferred_element_type=F32)
        gated = (jax.nn.sigmoid(ga_ref[rows, :].astype(F32)) * ya
                 + jax.nn.sigmoid(gb_ref[rows, :].astype(F32)) * yb)
        merged.append(gated.astype(BF16))
    for r0, mrg in zip(range(0, MERGE_TM, MERGE_SUB), merged):
        rows = slice(r0, r0 + MERGE_SUB)
        o_ref[rows, :] = x_ref[rows, :] + jnp.dot(mrg, wo_ref[...], preferred_element_type=F32)


def _merge(oat, ob, proj, x2, wa, wb, wo):
    m = x2.shape[0]
    tiles_per_seq = oat.shape[2] // MERGE_TM
    resident = functools.partial(pl.BlockSpec, index_map=lambda i: (0, 0), pipeline_mode=pl.Buffered(1))
    return pl.pallas_call(
        _merge_kernel,
        grid=(m // MERGE_TM,),
        in_specs=[
            pl.BlockSpec((1, A_Q_W, MERGE_TM), lambda i: (i // tiles_per_seq, 0, i % tiles_per_seq)),
            pl.BlockSpec((B_HEADS, MERGE_TM, HEAD_DIM), lambda i: (0, i, 0)),
            pl.BlockSpec((MERGE_TM, D_MODEL), lambda i: (i, COL_GA // D_MODEL)),
            pl.BlockSpec((MERGE_TM, D_MODEL), lambda i: (i, COL_GB // D_MODEL)),
            pl.BlockSpec((MERGE_TM, D_MODEL), lambda i: (i, 0)),
            resident((A_Q_W, D_MODEL)),
            resident((B_W, D_MODEL)),
            resident((D_MODEL, D_MODEL)),
        ],
        out_specs=pl.BlockSpec((MERGE_TM, D_MODEL), lambda i: (i, 0)),
        out_shape=jax.ShapeDtypeStruct((m, D_MODEL), F32),
        compiler_params=pltpu.CompilerParams(
            dimension_semantics=("parallel",), vmem_limit_bytes=VMEM_LIMIT_BYTES),
        name="merge",
    )(oat, ob, proj, proj, x2, wa, wb, wo)


MLP_NF = D_FF // MLP_TF


def _mlp_kernel(x_ref, g_ref, wu_ref, wd_ref, o_ref, h_ref, a_ref):
    f = pl.program_id(1)

    def up(slot):
        u = jnp.dot(h_ref[...], wu_ref[...], preferred_element_type=F32)
        a_ref[slot] = jnp.square(jnp.maximum(u, 0.0)).astype(BF16)

    def down(slot):
        o_ref[...] += jnp.dot(a_ref[slot], wd_ref[...], preferred_element_type=F32)

    @pl.when(f == 0)
    def _():
        x = x_ref[...]
        h_ref[...] = (x * _rms_scale(x) * g_ref[...]).astype(BF16)
        o_ref[...] = x
        up(0)

    @pl.when(jnp.logical_and(f > 0, f < MLP_NF))
    def _():
        down((f - 1) % 2)
        up(f % 2)

    @pl.when(f == MLP_NF)
    def _():
        down((MLP_NF - 1) % 2)


def _mlp(x2, g, wu, wd):
    m = x2.shape[0]
    return pl.pallas_call(
        _mlp_kernel,
        grid=(m // MLP_TM, MLP_NF + 1),
        in_specs=[
            pl.BlockSpec((MLP_TM, D_MODEL), lambda i, f: (i, 0)),
            pl.BlockSpec((1, D_MODEL), lambda i, f: (0, 0)),
            pl.BlockSpec((D_MODEL, MLP_TF), lambda i, f: (0, jnp.minimum(f, MLP_NF - 1))),
            pl.BlockSpec((MLP_TF, D_MODEL), lambda i, f: (jnp.maximum(f - 1, 0), 0)),
        ],
        out_specs=pl.BlockSpec((MLP_TM, D_MODEL), lambda i, f: (i, 0)),
        out_shape=jax.ShapeDtypeStruct((m, D_MODEL), F32),
        scratch_shapes=[pltpu.VMEM((MLP_TM, D_MODEL), BF16), pltpu.VMEM((2, MLP_TM, MLP_TF), BF16)],
        compiler_params=pltpu.CompilerParams(
            dimension_semantics=("parallel", "arbitrary"), vmem_limit_bytes=VMEM_LIMIT_BYTES),
        name="mlp",
    )(x2, g, wu, wd)


def _t5_bucket(rel):
    nb = N_BUCKETS // 2
    max_exact = nb // 2
    ret = (rel > 0).astype(np.int32) * nb
    n = np.abs(rel).astype(np.int32)
    nf = np.maximum(n, max_exact).astype(np.float32)
    large = max_exact + (np.log(nf / max_exact) / np.log(MAX_DISTANCE / max_exact) * (nb - max_exact)).astype(np.int32)
    large = np.minimum(large, nb - 1)
    return ret + np.where(n < max_exact, n, large)


def _select_rows(onehot, table):
    return jnp.einsum("ij,j...->i...", jnp.asarray(onehot, F32), table.astype(F32), precision=lax.Precision.HIGHEST)


def _window_bias_table(t5_bias):
    n_off = 4 * BLK
    offsets = np.arange(n_off - 1) - (2 * BLK - 1)
    onehot = _t5_bucket(offsets)[:, None] == np.arange(N_BUCKETS)[None, :]
    per_offset = jnp.where((np.abs(offsets) <= WINDOW)[:, None], _select_rows(onehot, t5_bias), NEG)
    y = jnp.pad(per_offset.T, ((0, 0), (0, 1)))
    skew = jnp.tile(y, (1, BLK + 1))[:, :BLK * (n_off + 1)].reshape(A_Q_HEADS, BLK, n_off + 1)
    table = skew[:, ::-1, :3 * BLK].reshape(A_KV_HEADS, A_GROUP, BLK, 3 * BLK)
    return jnp.transpose(table, (0, 3, 1, 2)).reshape(A_KV_HEADS, 3 * BLK, A_GROUP * BLK)


def _nbr_bias_table(rpb):
    qc = np.arange(GRID_W)[:, None]
    kc = np.arange(GRID_W)[None, :]
    start_c = np.clip(qc - NA_KW // 2, 0, GRID_W - NA_KW)
    col_mask = (kc >= start_c) & (kc < start_c + NA_KW)
    dcc = np.clip(kc - qc, -(NA_KW - 1), NA_KW - 1) + (NA_KW - 1)
    onehot = dcc.reshape(-1, 1) == np.arange(2 * NA_KW - 1)[None, :]
    by_col = _select_rows(onehot, jnp.transpose(rpb, (2, 0, 1)))
    by_col = jnp.where(col_mask.reshape(-1, 1, 1), by_col, NEG).reshape(GRID_W, GRID_W, B_HEADS, 2 * NA_KH - 1)
    by_col = jnp.transpose(by_col, (2, 0, 3, 1))
    variants = [by_col[:, :, NA_KH - 1 - d:2 * NA_KH - 1 - d, :] for d in range(NA_KH)]
    return jnp.stack(variants, axis=1).reshape(B_HEADS, NA_KH, GRID_W, NA_KH * GRID_W)


def _prepare(norm_mix_g, w_in, q_norm_a, k_norm_a, t5_bias, sink_a, q_norm_b, k_norm_b, rpb_b,
             w_br_a, w_br_b, w_out, norm_mlp_g, w_up, w_down):
    w = w_in[0]
    o_qa, o_ka, o_va, o_qb, o_kb, o_vb, o_ga, o_gb = np.cumsum(
        (0, A_Q_W, A_KV_W, A_KV_W, B_W, B_W, B_W, D_MODEL))
    w_perm = jnp.concatenate([
        w[:, o_ga:o_gb], w[:, o_gb:], w[:, o_qa:o_ka], w[:, o_ka:o_va], w[:, o_va:o_qb],
        w[:, o_qb:o_kb], w[:, o_kb:o_vb], w[:, o_vb:o_ga]], axis=1).astype(BF16)
    scale = HEAD_DIM ** -0.5
    colgain = jnp.concatenate([
        jnp.ones((2 * D_MODEL,), F32),
        jnp.tile(q_norm_a[0].astype(F32), A_Q_HEADS) * scale,
        jnp.tile(k_norm_a[0].astype(F32), A_KV_HEADS),
        jnp.ones((A_KV_W,), F32),
        jnp.tile(q_norm_b[0].astype(F32), B_HEADS) * scale,
        jnp.tile(k_norm_b[0].astype(F32), B_HEADS),
        jnp.ones((B_W,), F32)]).reshape(1, IN_WIDTH)
    return dict(
        g_mix=norm_mix_g[0].astype(F32).reshape(1, D_MODEL),
        w_in=w_perm,
        colgain=colgain,
        sink=sink_a[0].astype(F32),
        bias_a=_window_bias_table(t5_bias),
        bias_b=_nbr_bias_table(rpb_b[0]),
        w_a=w_br_a[0].astype(BF16),
        w_b=w_br_b[0].astype(BF16),
        w_o=w_out[0].astype(BF16),
        g_mlp=norm_mlp_g[0].astype(F32).reshape(1, D_MODEL),
        w_up=w_up[0].astype(BF16),
        w_down=w_down[0].astype(BF16),
    )


def _layer(x, p):
    b, seq, _ = x.shape
    x2 = x.reshape(b * seq, D_MODEL)
    proj, heads_b = _in_proj(x2, p["g_mix"], p["w_in"], p["colgain"])
    proj3 = proj.reshape(b, seq, PROJ_W)
    oat = _win_attn(proj3, p["sink"], p["bias_a"])
    ob = _nbr_attn(heads_b, b, seq, p["bias_b"])
    x1 = _merge(oat, ob, proj, x2, p["w_a"], p["w_b"], p["w_o"])
    y = _mlp(x1, p["g_mlp"], p["w_up"], p["w_down"])
    return y.reshape(b, seq, D_MODEL)


def kernel(x_prompt, x_sample, norm_mix_g, w_in, q_norm_a, k_norm_a, t5_bias, sink_a, q_norm_b, k_norm_b, rpb_b,
           w_br_a, w_br_b, w_out, norm_mlp_g, w_up, w_down):
    p = _prepare(norm_mix_g, w_in, q_norm_a, k_norm_a, t5_bias, sink_a, q_norm_b, k_norm_b, rpb_b,
                 w_br_a, w_br_b, w_out, norm_mlp_g, w_up, w_down)
    return (_layer(x_prompt, p), _layer(x_sample, p))
```

```python
import functools

import jax
import jax.numpy as jnp
import numpy as np
from jax import lax
from jax.experimental import pallas as pl
from jax.experimental.pallas import tpu as pltpu

D_MODEL = 2048
HEAD_DIM = 128
A_Q_HEADS = 8
A_KV_HEADS = 2
A_GROUP = A_Q_HEADS // A_KV_HEADS
WINDOW = 128
BLK = WINDOW
N_BUCKETS = 32
MAX_DISTANCE = 128
B_HEADS = 8
GRID_W = 64
NA_KH = 8
NA_KW = 16
D_FF = 4 * D_MODEL
EPS = 1e-6
NEG = -1e30
LOG2E = float(np.log2(np.e))

A_Q_W = A_Q_HEADS * HEAD_DIM
A_KV_W = A_KV_HEADS * HEAD_DIM
B_W = B_HEADS * HEAD_DIM
IN_WIDTH = A_Q_W + 2 * A_KV_W + 3 * B_W + 2 * D_MODEL

SRC_QA = 0
SRC_KVA = SRC_QA + A_Q_W
SRC_QB = SRC_KVA + 2 * A_KV_W
SRC_KB = SRC_QB + B_W
SRC_VB = SRC_KB + B_W
SRC_GA = SRC_VB + B_W
SRC_GB = SRC_GA + D_MODEL
assert SRC_GB + D_MODEL == IN_WIDTH

COL_GA = 0
COL_GB = COL_GA + D_MODEL
COL_QA = COL_GB + D_MODEL
COL_KVA = COL_QA + A_Q_W
PROJ_W = COL_KVA + 2 * A_KV_W

V7X_VMEM_BYTES = 64 * 1024 * 1024
VMEM_LIMIT_BYTES = V7X_VMEM_BYTES - 8 * 1024 * 1024

PROJ_TM = 256
PROJ_TN = 512
WIN_TQ = 4 * BLK
WIN_LOOKAHEAD = 1
NBR_LOOKAHEAD = 3
MERGE_TM = 512
MERGE_SUB = 256
MLP_TM = 1024
MLP_TF = 512

BF16 = jnp.bfloat16
F32 = jnp.float32


def _rms_scale(x):
    return lax.rsqrt(jnp.mean(x * x, axis=-1, keepdims=True) + EPS)


def _issue_ahead(n, depth, first_stage, second_stage):
    pending = [first_stage(i) for i in range(min(depth, n))]
    for i in range(n):
        if i + depth < n:
            pending.append(first_stage(i + depth))
        second_stage(i, pending.pop(0))


def _head_rmsnorm(a):
    parts = []
    for c in range(a.shape[1] // HEAD_DIM):
        blk = a[:, c * HEAD_DIM:(c + 1) * HEAD_DIM]
        parts.append(blk * _rms_scale(blk))
    return jnp.concatenate(parts, axis=1)


def _in_proj_kernel(x_ref, g_ref, w_ref, cg_ref, o_ref, hb_ref, h_ref):
    x = x_ref[...]
    h_ref[...] = (x * _rms_scale(x) * g_ref[...]).astype(BF16)
    for c0 in range(0, IN_WIDTH, PROJ_TN):
        cols = slice(c0, c0 + PROJ_TN)
        acc = jnp.dot(h_ref[...], w_ref[:, cols], preferred_element_type=F32)
        if c0 < SRC_KVA or SRC_QB <= c0 < SRC_VB:
            acc = _head_rmsnorm(acc) * cg_ref[:, cols]
        elif c0 == SRC_KVA:
            k = _head_rmsnorm(acc[:, :A_KV_W]) * cg_ref[:, c0:c0 + A_KV_W]
            acc = jnp.concatenate([k, acc[:, A_KV_W:]], axis=1)
        if SRC_QB <= c0 < SRC_GA:
            for j in range(PROJ_TN // HEAD_DIM):
                hb_ref[(c0 - SRC_QB) // HEAD_DIM + j] = acc[:, j * HEAD_DIM:(j + 1) * HEAD_DIM].astype(BF16)
        else:
            if c0 < SRC_KVA:
                d0 = COL_QA + c0 - SRC_QA
            elif c0 == SRC_KVA:
                d0 = COL_KVA
            elif c0 < SRC_GB:
                d0 = COL_GA + c0 - SRC_GA
            else:
                d0 = COL_GB + c0 - SRC_GB
            o_ref[:, d0:d0 + PROJ_TN] = acc.astype(BF16)


def _in_proj(x2, g, w, colgain):
    m = x2.shape[0]
    resident = functools.partial(pl.BlockSpec, index_map=lambda i: (0, 0), pipeline_mode=pl.Buffered(1))
    return pl.pallas_call(
        _in_proj_kernel,
        grid=(m // PROJ_TM,),
        in_specs=[
            pl.BlockSpec((PROJ_TM, D_MODEL), lambda i: (i, 0)),
            resident((1, D_MODEL)),
            resident((D_MODEL, IN_WIDTH)),
            resident((1, IN_WIDTH)),
        ],
        out_specs=[
            pl.BlockSpec((PROJ_TM, PROJ_W), lambda i: (i, 0)),
            pl.BlockSpec((3 * B_HEADS, PROJ_TM, HEAD_DIM), lambda i: (0, i, 0)),
        ],
        out_shape=[
            jax.ShapeDtypeStruct((m, PROJ_W), BF16),
            jax.ShapeDtypeStruct((3 * B_HEADS, m, HEAD_DIM), BF16),
        ],
        scratch_shapes=[pltpu.VMEM((PROJ_TM, D_MODEL), BF16)],
        compiler_params=pltpu.CompilerParams(
            dimension_semantics=("parallel",), vmem_limit_bytes=VMEM_LIMIT_BYTES),
        name="in_proj",
    )(x2, g, w, colgain)


def _win_attn_kernel(sink_ref, q_ref, kvp_ref, kvm_ref, kvn_ref, bias_ref, o_ref, *, n_tiles):
    i = pl.program_id(1)
    kv = jnp.concatenate([kvp_ref[0], kvm_ref[0], kvn_ref[0]], axis=0)
    key = lax.broadcasted_iota(jnp.int32, (3 * BLK, 1), 0)
    first_key = jnp.where(i == 0, BLK, 0)
    end_key = jnp.where(i == n_tiles - 1, 2 * BLK, 3 * BLK)
    n_blk = WIN_TQ // BLK
    units = [(t, kvh) for t in range(n_blk) for kvh in range(A_KV_HEADS)]

    def scores(t, kvh):
        k = kv[t * BLK:(t + 3) * BLK, kvh * HEAD_DIM:(kvh + 1) * HEAD_DIM]
        q = jnp.concatenate(
            [q_ref[0, t * BLK:(t + 1) * BLK, (kvh * A_GROUP + g) * HEAD_DIM:(kvh * A_GROUP + g + 1) * HEAD_DIM]
             for g in range(A_GROUP)], axis=0)
        s = lax.dot_general(k, q, (((1,), (1,)), ((), ())), preferred_element_type=F32)
        s = s + bias_ref[kvh]
        if t == 0:
            s = jnp.where(key >= first_key, s, NEG)
        if t == n_blk - 1:
            s = jnp.where(key < end_key, s, NEG)
        return s, jnp.max(s, axis=0, keepdims=True)

    def finish(t, kvh, s_and_max):
        s, col_max = s_and_max
        heads = [kvh * A_GROUP + g for g in range(A_GROUP)]
        v = kv[t * BLK:(t + 3) * BLK, A_KV_W + kvh * HEAD_DIM:A_KV_W + (kvh + 1) * HEAD_DIM]
        sink = jnp.concatenate([jnp.full((1, BLK), sink_ref[h], F32) for h in heads], axis=1)
        m = jnp.maximum(col_max, sink)
        p = jnp.exp2(s - m)
        denom = jnp.sum(p, axis=0, keepdims=True) + jnp.exp2(sink - m)
        o = lax.dot_general(v, p.astype(BF16), (((0,), (0,)), ((), ())), preferred_element_type=F32) / denom
        for g, h in enumerate(heads):
            o_ref[0, h * HEAD_DIM:(h + 1) * HEAD_DIM, t * BLK:(t + 1) * BLK] = (
                o[:, g * BLK:(g + 1) * BLK].astype(BF16))

    _issue_ahead(len(units), WIN_LOOKAHEAD, lambda u: scores(*units[u]), lambda u, s: finish(*units[u], s))


def _win_attn(proj3, sink, bias_a):
    b, seq, _ = proj3.shape
    n_tiles = seq // WIN_TQ
    blk_per_tile = WIN_TQ // BLK
    last_blk = seq // BLK - 1
    kva_wide = COL_KVA // (2 * A_KV_W)
    return pl.pallas_call(
        functools.partial(_win_attn_kernel, n_tiles=n_tiles),
        grid=(b, n_tiles),
        in_specs=[
            pl.BlockSpec(memory_space=pltpu.SMEM),
            pl.BlockSpec((1, WIN_TQ, A_Q_W), lambda bi, i: (bi, i, COL_QA // A_Q_W)),
            pl.BlockSpec((1, BLK, 2 * A_KV_W),
                         lambda bi, i: (bi, jnp.maximum(i * blk_per_tile - 1, 0), kva_wide)),
            pl.BlockSpec((1, WIN_TQ, 2 * A_KV_W), lambda bi, i: (bi, i, kva_wide)),
            pl.BlockSpec((1, BLK, 2 * A_KV_W),
                         lambda bi, i: (bi, jnp.minimum((i + 1) * blk_per_tile, last_blk), kva_wide)),
            pl.BlockSpec((A_KV_HEADS, 3 * BLK, A_GROUP * BLK), lambda bi, i: (0, 0, 0)),
        ],
        out_specs=pl.BlockSpec((1, A_Q_W, WIN_TQ), lambda bi, i: (bi, 0, i)),
        out_shape=jax.ShapeDtypeStruct((b, A_Q_W, seq), BF16),
        compiler_params=pltpu.CompilerParams(
            dimension_semantics=("parallel", "arbitrary"), vmem_limit_bytes=VMEM_LIMIT_BYTES),
        name="win_attn",
    )(sink, proj3, proj3, proj3, proj3, bias_a)


def _nbr_attn_kernel(q_ref, k_ref, v_ref, bias_ref, o_ref, *, rows):
    def first_key_row(r):
        return min(max(r - NA_KH // 2, 0), rows - NA_KH)

    def scores(r):
        k0 = first_key_row(r) * GRID_W
        q = q_ref[0, r * GRID_W:(r + 1) * GRID_W, :]
        k = k_ref[0, k0:k0 + NA_KH * GRID_W, :]
        s = lax.dot_general(q, k, (((1,), (1,)), ((), ())), preferred_element_type=F32)
        s = s + bias_ref[0, r - k0 // GRID_W]
        return s, jnp.max(s, axis=-1, keepdims=True)

    def finish(r, s_and_max):
        s, m = s_and_max
        rs = first_key_row(r)
        v = v_ref[0, rs * GRID_W:(rs + NA_KH) * GRID_W, :]
        p = jnp.exp(s - m)
        denom = jnp.sum(p, axis=-1, keepdims=True)
        o = jnp.dot(p.astype(BF16), v, preferred_element_type=F32) / denom
        o_ref[0, r * GRID_W:(r + 1) * GRID_W, :] = o.astype(BF16)

    _issue_ahead(rows, NBR_LOOKAHEAD, scores, finish)


def _nbr_attn(heads_b, b, seq, bias_b):
    rows = seq // GRID_W
    return pl.pallas_call(
        functools.partial(_nbr_attn_kernel, rows=rows),
        grid=(b, B_HEADS),
        in_specs=[
            pl.BlockSpec((1, seq, HEAD_DIM), lambda bi, h: (h, bi, 0)),
            pl.BlockSpec((1, seq, HEAD_DIM), lambda bi, h: (B_HEADS + h, bi, 0)),
            pl.BlockSpec((1, seq, HEAD_DIM), lambda bi, h: (2 * B_HEADS + h, bi, 0)),
            pl.BlockSpec((1, NA_KH, GRID_W, NA_KH * GRID_W), lambda bi, h: (h, 0, 0, 0)),
        ],
        out_specs=pl.BlockSpec((1, seq, HEAD_DIM), lambda bi, h: (h, bi, 0)),
        out_shape=jax.ShapeDtypeStruct((B_HEADS, b * seq, HEAD_DIM), BF16),
        compiler_params=pltpu.CompilerParams(
            dimension_semantics=("parallel", "arbitrary"), vmem_limit_bytes=VMEM_LIMIT_BYTES),
        name="nbr_attn",
    )(heads_b, heads_b, heads_b, bias_b)


def _merge_kernel(oat_ref, ob_ref, ga_ref, gb_ref, x_ref, wa_ref, wb_ref, wo_ref, o_ref):
    merged = []
    for r0 in range(0, MERGE_TM, MERGE_SUB):
        rows = slice(r0, r0 + MERGE_SUB)
        ya = lax.dot_general(oat_ref[0, :, rows], wa_ref[...], (((0,), (0,)), ((), ())), preferred_element_type=F32)
        ob = jnp.concatenate([ob_ref[h, rows, :] for h in range(B_HEADS)], axis=1)
        yb = jnp.dot(ob, wb_ref[...], preferred_element_type=F32)
        gated = (jax.nn.sigmoid(ga_ref[rows, :].astype(F32)) * ya
                 + jax.nn.sigmoid(gb_ref[rows, :].astype(F32)) * yb)
        merged.append(gated.astype(BF16))
    for r0, mrg in zip(range(0, MERGE_TM, MERGE_SUB), merged):
        rows = slice(r0, r0 + MERGE_SUB)
        o_ref[rows, :] = x_ref[rows, :] + jnp.dot(mrg, wo_ref[...], preferred_element_type=F32)


def _merge(oat, ob, proj, x2, wa, wb, wo):
    m = x2.shape[0]
    tiles_per_seq = oat.shape[2] // MERGE_TM
    resident = functools.partial(pl.BlockSpec, index_map=lambda i: (0, 0), pipeline_mode=pl.Buffered(1))
    return pl.pallas_call(
        _merge_kernel,
        grid=(m // MERGE_TM,),
        in_specs=[
            pl.BlockSpec((1, A_Q_W, MERGE_TM), lambda i: (i // tiles_per_seq, 0, i % tiles_per_seq)),
            pl.BlockSpec((B_HEADS, MERGE_TM, HEAD_DIM), lambda i: (0, i, 0)),
            pl.BlockSpec((MERGE_TM, D_MODEL), lambda i: (i, COL_GA // D_MODEL)),
            pl.BlockSpec((MERGE_TM, D_MODEL), lambda i: (i, COL_GB // D_MODEL)),
            pl.BlockSpec((MERGE_TM, D_MODEL), lambda i: (i, 0)),
            resident((A_Q_W, D_MODEL)),
            resident((B_W, D_MODEL)),
            resident((D_MODEL, D_MODEL)),
        ],
        out_specs=pl.BlockSpec((MERGE_TM, D_MODEL), lambda i: (i, 0)),
        out_shape=jax.ShapeDtypeStruct((m, D_MODEL), F32),
        compiler_params=pltpu.CompilerParams(
            dimension_semantics=("parallel",), vmem_limit_bytes=VMEM_LIMIT_BYTES),
        name="merge",
    )(oat, ob, proj, proj, x2, wa, wb, wo)


def _mlp_kernel(x_ref, g_ref, wu_ref, wd_ref, o_ref, h_ref):
    @pl.when(pl.program_id(1) == 0)
    def _():
        x = x_ref[...]
        h_ref[...] = (x * _rms_scale(x) * g_ref[...]).astype(BF16)
        o_ref[...] = x

    u = jnp.dot(h_ref[...], wu_ref[...], preferred_element_type=F32)
    a = jnp.square(jnp.maximum(u, 0.0)).astype(BF16)
    o_ref[...] += jnp.dot(a, wd_ref[...], preferred_element_type=F32)


def _mlp(x2, g, wu, wd):
    m = x2.shape[0]
    return pl.pallas_call(
        _mlp_kernel,
        grid=(m // MLP_TM, D_FF // MLP_TF),
        in_specs=[
            pl.BlockSpec((MLP_TM, D_MODEL), lambda i, f: (i, 0)),
            pl.BlockSpec((1, D_MODEL), lambda i, f: (0, 0)),
            pl.BlockSpec((D_MODEL, MLP_TF), lambda i, f: (0, f)),
            pl.BlockSpec((MLP_TF, D_MODEL), lambda i, f: (f, 0)),
        ],
        out_specs=pl.BlockSpec((MLP_TM, D_MODEL), lambda i, f: (i, 0)),
        out_shape=jax.ShapeDtypeStruct((m, D_MODEL), F32),
        scratch_shapes=[pltpu.VMEM((MLP_TM, D_MODEL), BF16)],
        compiler_params=pltpu.CompilerParams(
            dimension_semantics=("parallel", "arbitrary"), vmem_limit_bytes=VMEM_LIMIT_BYTES),
        name="mlp",
    )(x2, g, wu, wd)


def _t5_bucket(rel):
    nb = N_BUCKETS // 2
    max_exact = nb // 2
    ret = (rel > 0).astype(np.int32) * nb
    n = np.abs(rel).astype(np.int32)
    nf = np.maximum(n, max_exact).astype(np.float32)
    large = max_exact + (np.log(nf / max_exact) / np.log(MAX_DISTANCE / max_exact) * (nb - max_exact)).astype(np.int32)
    large = np.minimum(large, nb - 1)
    return ret + np.where(n < max_exact, n, large)


def _select_rows(onehot, table):
    return jnp.einsum("ij,j...->i...", jnp.asarray(onehot, F32), table.astype(F32), precision=lax.Precision.HIGHEST)


def _window_bias_table(t5_bias):
    n_off = 4 * BLK
    offsets = np.arange(n_off - 1) - (2 * BLK - 1)
    onehot = _t5_bucket(offsets)[:, None] == np.arange(N_BUCKETS)[None, :]
    per_offset = jnp.where((np.abs(offsets) <= WINDOW)[:, None], _select_rows(onehot, t5_bias), NEG)
    y = jnp.pad(per_offset.T, ((0, 0), (0, 1)))
    skew = jnp.tile(y, (1, BLK + 1))[:, :BLK * (n_off + 1)].reshape(A_Q_HEADS, BLK, n_off + 1)
    table = skew[:, ::-1, :3 * BLK].reshape(A_KV_HEADS, A_GROUP, BLK, 3 * BLK)
    return jnp.transpose(table, (0, 3, 1, 2)).reshape(A_KV_HEADS, 3 * BLK, A_GROUP * BLK)


def _nbr_bias_table(rpb):
    qc = np.arange(GRID_W)[:, None]
    kc = np.arange(GRID_W)[None, :]
    start_c = np.clip(qc - NA_KW // 2, 0, GRID_W - NA_KW)
    col_mask = (kc >= start_c) & (kc < start_c + NA_KW)
    dcc = np.clip(kc - qc, -(NA_KW - 1), NA_KW - 1) + (NA_KW - 1)
    onehot = dcc.reshape(-1, 1) == np.arange(2 * NA_KW - 1)[None, :]
    by_col = _select_rows(onehot, jnp.transpose(rpb, (2, 0, 1)))
    by_col = jnp.where(col_mask.reshape(-1, 1, 1), by_col, NEG).reshape(GRID_W, GRID_W, B_HEADS, 2 * NA_KH - 1)
    by_col = jnp.transpose(by_col, (2, 0, 3, 1))
    variants = [by_col[:, :, NA_KH - 1 - d:2 * NA_KH - 1 - d, :] for d in range(NA_KH)]
    return jnp.stack(variants, axis=1).reshape(B_HEADS, NA_KH, GRID_W, NA_KH * GRID_W)


def _prepare(norm_mix_g, w_in, q_norm_a, k_norm_a, t5_bias, sink_a, q_norm_b, k_norm_b, rpb_b,
             w_br_a, w_br_b, w_out, norm_mlp_g, w_up, w_down):
    scale = HEAD_DIM ** -0.5
    colgain = jnp.concatenate([
        jnp.tile(q_norm_a[0].astype(F32), A_Q_HEADS) * (scale * LOG2E),
        jnp.tile(k_norm_a[0].astype(F32), A_KV_HEADS),
        jnp.ones((A_KV_W,), F32),
        jnp.tile(q_norm_b[0].astype(F32), B_HEADS) * scale,
        jnp.tile(k_norm_b[0].astype(F32), B_HEADS),
        jnp.ones((B_W + 2 * D_MODEL,), F32)]).reshape(1, IN_WIDTH)
    return dict(
        g_mix=norm_mix_g[0].astype(F32).reshape(1, D_MODEL),
        w_in=w_in[0].astype(BF16),
        colgain=colgain,
        sink=sink_a[0].astype(F32) * LOG2E,
        bias_a=_window_bias_table(t5_bias * LOG2E),
        bias_b=_nbr_bias_table(rpb_b[0]),
        w_a=w_br_a[0].astype(BF16),
        w_b=w_br_b[0].astype(BF16),
        w_o=w_out[0].astype(BF16),
        g_mlp=norm_mlp_g[0].astype(F32).reshape(1, D_MODEL),
        w_up=w_up[0].astype(BF16),
        w_down=w_down[0].astype(BF16),
    )


def _layer(x, p):
    b, seq, _ = x.shape
    x2 = x.reshape(b * seq, D_MODEL)
    proj, heads_b = _in_proj(x2, p["g_mix"], p["w_in"], p["colgain"])
    proj3 = proj.reshape(b, seq, PROJ_W)
    oat = _win_attn(proj3, p["sink"], p["bias_a"])
    ob = _nbr_attn(heads_b, b, seq, p["bias_b"])
    x1 = _merge(oat, ob, proj, x2, p["w_a"], p["w_b"], p["w_o"])
    y = _mlp(x1, p["g_mlp"], p["w_up"], p["w_down"])
    return y.reshape(b, seq, D_MODEL)


def kernel(x_prompt, x_sample, norm_mix_g, w_in, q_norm_a, k_norm_a, t5_bias, sink_a, q_norm_b, k_norm_b, rpb_b,
           w_br_a, w_br_b, w_out, norm_mlp_g, w_up, w_down):
    p = _prepare(norm_mix_g, w_in, q_norm_a, k_norm_a, t5_bias, sink_a, q_norm_b, k_norm_b, rpb_b,
                 w_br_a, w_br_b, w_out, norm_mlp_g, w_up, w_down)
    return (_layer(x_prompt, p), _layer(x_sample, p))
```

```python
import functools

import jax
import jax.numpy as jnp
import numpy as np
from jax import lax
from jax.experimental import pallas as pl
from jax.experimental.pallas import tpu as pltpu

D_MODEL = 2048
HEAD_DIM = 128
A_Q_HEADS = 8
A_KV_HEADS = 2
A_GROUP = A_Q_HEADS // A_KV_HEADS
WINDOW = 128
BLK = WINDOW
N_BUCKETS = 32
MAX_DISTANCE = 128
B_HEADS = 8
GRID_W = 64
NA_KH = 8
NA_KW = 16
D_FF = 4 * D_MODEL
EPS = 1e-6
NEG = -1e30
LOG2E = float(np.log2(np.e))

A_Q_W = A_Q_HEADS * HEAD_DIM
A_KV_W = A_KV_HEADS * HEAD_DIM
B_W = B_HEADS * HEAD_DIM
IN_WIDTH = A_Q_W + 2 * A_KV_W + 3 * B_W + 2 * D_MODEL

SRC_QA = 0
SRC_KVA = SRC_QA + A_Q_W
SRC_QB = SRC_KVA + 2 * A_KV_W
SRC_KB = SRC_QB + B_W
SRC_VB = SRC_KB + B_W
SRC_GA = SRC_VB + B_W
SRC_GB = SRC_GA + D_MODEL
assert SRC_GB + D_MODEL == IN_WIDTH

COL_GA = 0
COL_GB = COL_GA + D_MODEL
COL_QA = COL_GB + D_MODEL
COL_KVA = COL_QA + A_Q_W
PROJ_W = COL_KVA + 2 * A_KV_W

V7X_VMEM_BYTES = 64 * 1024 * 1024
VMEM_LIMIT_BYTES = V7X_VMEM_BYTES - 8 * 1024 * 1024

PROJ_TM = 256
PROJ_TN = 512
WIN_TQ = 4 * BLK
WIN_LOOKAHEAD = 1
NBR_LOOKAHEAD = 5
MERGE_TM = 512
MERGE_SUB = 256
MLP_TM = 1024
MLP_TF = 512

BF16 = jnp.bfloat16
F32 = jnp.float32


def _rms_scale(x):
    return lax.rsqrt(jnp.mean(x * x, axis=-1, keepdims=True) + EPS)


def _issue_ahead(n, depth, first_stage, second_stage):
    pending = [first_stage(i) for i in range(min(depth, n))]
    for i in range(n):
        if i + depth < n:
            pending.append(first_stage(i + depth))
        second_stage(i, pending.pop(0))


def _head_rmsnorm(a):
    parts = []
    for c in range(a.shape[1] // HEAD_DIM):
        blk = a[:, c * HEAD_DIM:(c + 1) * HEAD_DIM]
        parts.append(blk * _rms_scale(blk))
    return jnp.concatenate(parts, axis=1)


def _in_proj_kernel(x_ref, g_ref, w_ref, cg_ref, o_ref, qvb_ref, kbt_ref, h_ref):
    x = x_ref[...]
    h_ref[...] = (x * _rms_scale(x) * g_ref[...]).astype(BF16)
    for c0 in range(0, IN_WIDTH, PROJ_TN):
        cols = slice(c0, c0 + PROJ_TN)
        acc = jnp.dot(h_ref[...], w_ref[:, cols], preferred_element_type=F32)
        if c0 < SRC_KVA or SRC_QB <= c0 < SRC_VB:
            acc = _head_rmsnorm(acc) * cg_ref[:, cols]
        elif c0 == SRC_KVA:
            k = _head_rmsnorm(acc[:, :A_KV_W]) * cg_ref[:, c0:c0 + A_KV_W]
            acc = jnp.concatenate([k, acc[:, A_KV_W:]], axis=1)
        if SRC_QB <= c0 < SRC_GA:
            for j in range(PROJ_TN // HEAD_DIM):
                head = acc[:, j * HEAD_DIM:(j + 1) * HEAD_DIM]
                if c0 < SRC_KB:
                    qvb_ref[(c0 - SRC_QB) // HEAD_DIM + j] = head.astype(BF16)
                elif c0 < SRC_VB:
                    kbt_ref[(c0 - SRC_KB) // HEAD_DIM + j] = head.T.astype(BF16)
                else:
                    qvb_ref[B_HEADS + (c0 - SRC_VB) // HEAD_DIM + j] = head.astype(BF16)
        else:
            if c0 < SRC_KVA:
                d0 = COL_QA + c0 - SRC_QA
            elif c0 == SRC_KVA:
                d0 = COL_KVA
            elif c0 < SRC_GB:
                d0 = COL_GA + c0 - SRC_GA
            else:
                d0 = COL_GB + c0 - SRC_GB
            o_ref[:, d0:d0 + PROJ_TN] = acc.astype(BF16)


def _in_proj(x2, g, w, colgain):
    m = x2.shape[0]
    resident = functools.partial(pl.BlockSpec, index_map=lambda i: (0, 0), pipeline_mode=pl.Buffered(1))
    return pl.pallas_call(
        _in_proj_kernel,
        grid=(m // PROJ_TM,),
        in_specs=[
            pl.BlockSpec((PROJ_TM, D_MODEL), lambda i: (i, 0)),
            resident((1, D_MODEL)),
            resident((D_MODEL, IN_WIDTH)),
            resident((1, IN_WIDTH)),
        ],
        out_specs=[
            pl.BlockSpec((PROJ_TM, PROJ_W), lambda i: (i, 0)),
            pl.BlockSpec((2 * B_HEADS, PROJ_TM, HEAD_DIM), lambda i: (0, i, 0)),
            pl.BlockSpec((B_HEADS, HEAD_DIM, PROJ_TM), lambda i: (0, 0, i)),
        ],
        out_shape=[
            jax.ShapeDtypeStruct((m, PROJ_W), BF16),
            jax.ShapeDtypeStruct((2 * B_HEADS, m, HEAD_DIM), BF16),
            jax.ShapeDtypeStruct((B_HEADS, HEAD_DIM, m), BF16),
        ],
        scratch_shapes=[pltpu.VMEM((PROJ_TM, D_MODEL), BF16)],
        compiler_params=pltpu.CompilerParams(
            dimension_semantics=("parallel",), vmem_limit_bytes=VMEM_LIMIT_BYTES),
        name="in_proj",
    )(x2, g, w, colgain)


def _win_attn_kernel(sink_ref, q_ref, kvp_ref, kvm_ref, kvn_ref, bias_ref, o_ref, *, n_tiles):
    i = pl.program_id(1)
    kv = jnp.concatenate([kvp_ref[0], kvm_ref[0], kvn_ref[0]], axis=0)
    key = lax.broadcasted_iota(jnp.int32, (3 * BLK, 1), 0)
    first_key = jnp.where(i == 0, BLK, 0)
    end_key = jnp.where(i == n_tiles - 1, 2 * BLK, 3 * BLK)
    n_blk = WIN_TQ // BLK
    units = [(t, kvh) for t in range(n_blk) for kvh in range(A_KV_HEADS)]

    def scores(t, kvh):
        k = kv[t * BLK:(t + 3) * BLK, kvh * HEAD_DIM:(kvh + 1) * HEAD_DIM]
        q = jnp.concatenate(
            [q_ref[0, t * BLK:(t + 1) * BLK, (kvh * A_GROUP + g) * HEAD_DIM:(kvh * A_GROUP + g + 1) * HEAD_DIM]
             for g in range(A_GROUP)], axis=0)
        s = lax.dot_general(k, q, (((1,), (1,)), ((), ())), preferred_element_type=F32)
        s = s + bias_ref[kvh]
        if t == 0:
            s = jnp.where(key >= first_key, s, NEG)
        if t == n_blk - 1:
            s = jnp.where(key < end_key, s, NEG)
        return s, jnp.max(s, axis=0, keepdims=True)

    def finish(t, kvh, s_and_max):
        s, col_max = s_and_max
        heads = [kvh * A_GROUP + g for g in range(A_GROUP)]
        v = kv[t * BLK:(t + 3) * BLK, A_KV_W + kvh * HEAD_DIM:A_KV_W + (kvh + 1) * HEAD_DIM]
        sink = jnp.concatenate([jnp.full((1, BLK), sink_ref[h], F32) for h in heads], axis=1)
        m = jnp.maximum(col_max, sink)
        p = jnp.exp2(s - m)
        denom = jnp.sum(p, axis=0, keepdims=True) + jnp.exp2(sink - m)
        o = lax.dot_general(v, p.astype(BF16), (((0,), (0,)), ((), ())), preferred_element_type=F32) / denom
        for g, h in enumerate(heads):
            o_ref[0, h * HEAD_DIM:(h + 1) * HEAD_DIM, t * BLK:(t + 1) * BLK] = (
                o[:, g * BLK:(g + 1) * BLK].astype(BF16))

    _issue_ahead(len(units), WIN_LOOKAHEAD, lambda u: scores(*units[u]), lambda u, s: finish(*units[u], s))


def _win_attn(proj3, sink, bias_a):
    b, seq, _ = proj3.shape
    n_tiles = seq // WIN_TQ
    blk_per_tile = WIN_TQ // BLK
    last_blk = seq // BLK - 1
    kva_wide = COL_KVA // (2 * A_KV_W)
    return pl.pallas_call(
        functools.partial(_win_attn_kernel, n_tiles=n_tiles),
        grid=(b, n_tiles),
        in_specs=[
            pl.BlockSpec(memory_space=pltpu.SMEM),
            pl.BlockSpec((1, WIN_TQ, A_Q_W), lambda bi, i: (bi, i, COL_QA // A_Q_W)),
            pl.BlockSpec((1, BLK, 2 * A_KV_W),
                         lambda bi, i: (bi, jnp.maximum(i * blk_per_tile - 1, 0), kva_wide)),
            pl.BlockSpec((1, WIN_TQ, 2 * A_KV_W), lambda bi, i: (bi, i, kva_wide)),
            pl.BlockSpec((1, BLK, 2 * A_KV_W),
                         lambda bi, i: (bi, jnp.minimum((i + 1) * blk_per_tile, last_blk), kva_wide)),
            pl.BlockSpec((A_KV_HEADS, 3 * BLK, A_GROUP * BLK), lambda bi, i: (0, 0, 0)),
        ],
        out_specs=pl.BlockSpec((1, A_Q_W, WIN_TQ), lambda bi, i: (bi, 0, i)),
        out_shape=jax.ShapeDtypeStruct((b, A_Q_W, seq), BF16),
        compiler_params=pltpu.CompilerParams(
            dimension_semantics=("parallel", "arbitrary"), vmem_limit_bytes=VMEM_LIMIT_BYTES),
        name="win_attn",
    )(sink, proj3, proj3, proj3, proj3, bias_a)


def _nbr_attn_kernel(q_ref, kt_ref, v_ref, bias_ref, o_ref, *, rows):
    def first_key_row(r):
        return min(max(r - NA_KH // 2, 0), rows - NA_KH)

    kt_even = kt_ref[0]
    kt_odd = jnp.concatenate([kt_even[:, GRID_W:], kt_even[:, :GRID_W]], axis=1)

    def scores(r):
        rs = first_key_row(r)
        k0 = (rs - rs % 2) * GRID_W
        kt = (kt_odd if rs % 2 else kt_even)[:, k0:k0 + NA_KH * GRID_W]
        q = q_ref[0, r * GRID_W:(r + 1) * GRID_W, :]
        s = jnp.dot(q, kt, preferred_element_type=F32) + bias_ref[0, r - rs]
        return s, jnp.max(s, axis=-1, keepdims=True)

    def finish(r, s_and_max):
        s, m = s_and_max
        rs = first_key_row(r)
        v = v_ref[0, rs * GRID_W:(rs + NA_KH) * GRID_W, :]
        p = jnp.exp(s - m)
        denom = jnp.sum(p, axis=-1, keepdims=True)
        o = jnp.dot(p.astype(BF16), v, preferred_element_type=F32) / denom
        o_ref[0, r * GRID_W:(r + 1) * GRID_W, :] = o.astype(BF16)

    _issue_ahead(rows, NBR_LOOKAHEAD, scores, finish)


def _nbr_attn(qv_b, kt_b, b, seq, bias_b):
    rows = seq // GRID_W
    return pl.pallas_call(
        functools.partial(_nbr_attn_kernel, rows=rows),
        grid=(b, B_HEADS),
        in_specs=[
            pl.BlockSpec((1, seq, HEAD_DIM), lambda bi, h: (h, bi, 0)),
            pl.BlockSpec((1, HEAD_DIM, seq), lambda bi, h: (h, 0, bi)),
            pl.BlockSpec((1, seq, HEAD_DIM), lambda bi, h: (B_HEADS + h, bi, 0)),
            pl.BlockSpec((1, NA_KH, GRID_W, NA_KH * GRID_W), lambda bi, h: (h, 0, 0, 0)),
        ],
        out_specs=pl.BlockSpec((1, seq, HEAD_DIM), lambda bi, h: (h, bi, 0)),
        out_shape=jax.ShapeDtypeStruct((B_HEADS, b * seq, HEAD_DIM), BF16),
        compiler_params=pltpu.CompilerParams(
            dimension_semantics=("parallel", "arbitrary"), vmem_limit_bytes=VMEM_LIMIT_BYTES),
        name="nbr_attn",
    )(qv_b, kt_b, qv_b, bias_b)


def _merge_kernel(oat_ref, ob_ref, ga_ref, gb_ref, x_ref, wa_ref, wb_ref, wo_ref, o_ref):
    merged = []
    for r0 in range(0, MERGE_TM, MERGE_SUB):
        rows = slice(r0, r0 + MERGE_SUB)
        ya = lax.dot_general(oat_ref[0, :, rows], wa_ref[...], (((0,), (0,)), ((), ())), preferred_element_type=F32)
        ob = jnp.concatenate([ob_ref[h, rows, :] for h in range(B_HEADS)], axis=1)
        yb = jnp.dot(ob, wb_ref[...], preferred_element_type=F32)
        gated = (jax.nn.sigmoid(ga_ref[rows, :].astype(F32)) * ya
                 + jax.nn.sigmoid(gb_ref[rows, :].astype(F32)) * yb)
        merged.append(gated.astype(BF16))
    for r0, mrg in zip(range(0, MERGE_TM, MERGE_SUB), merged):
        rows = slice(r0, r0 + MERGE_SUB)
        o_ref[rows, :] = x_ref[rows, :] + jnp.dot(mrg, wo_ref[...], preferred_element_type=F32)


def _merge(oat, ob, proj, x2, wa, wb, wo):
    m = x2.shape[0]
    tiles_per_seq = oat.shape[2] // MERGE_TM
    resident = functools.partial(pl.BlockSpec, index_map=lambda i: (0, 0), pipeline_mode=pl.Buffered(1))
    return pl.pallas_call(
        _merge_kernel,
        grid=(m // MERGE_TM,),
        in_specs=[
            pl.BlockSpec((1, A_Q_W, MERGE_TM), lambda i: (i // tiles_per_seq, 0, i % tiles_per_seq)),
            pl.BlockSpec((B_HEADS, MERGE_TM, HEAD_DIM), lambda i: (0, i, 0)),
            pl.BlockSpec((MERGE_TM, D_MODEL), lambda i: (i, COL_GA // D_MODEL)),
            pl.BlockSpec((MERGE_TM, D_MODEL), lambda i: (i, COL_GB // D_MODEL)),
            pl.BlockSpec((MERGE_TM, D_MODEL), lambda i: (i, 0)),
            resident((A_Q_W, D_MODEL)),
            resident((B_W, D_MODEL)),
            resident((D_MODEL, D_MODEL)),
        ],
        out_specs=pl.BlockSpec((MERGE_TM, D_MODEL), lambda i: (i, 0)),
        out_shape=jax.ShapeDtypeStruct((m, D_MODEL), F32),
        compiler_params=pltpu.CompilerParams(
            dimension_semantics=("parallel",), vmem_limit_bytes=VMEM_LIMIT_BYTES),
        name="merge",
    )(oat, ob, proj, proj, x2, wa, wb, wo)


def _mlp_kernel(x_ref, g_ref, wu_ref, wd_ref, o_ref, h_ref):
    @pl.when(pl.program_id(1) == 0)
    def _():
        x = x_ref[...]
        h_ref[...] = (x * _rms_scale(x) * g_ref[...]).astype(BF16)
        o_ref[...] = x

    u = jnp.dot(h_ref[...], wu_ref[...], preferred_element_type=F32)
    a = jnp.square(jnp.maximum(u, 0.0)).astype(BF16)
    o_ref[...] += jnp.dot(a, wd_ref[...], preferred_element_type=F32)


def _mlp(x2, g, wu, wd):
    m = x2.shape[0]
    return pl.pallas_call(
        _mlp_kernel,
        grid=(m // MLP_TM, D_FF // MLP_TF),
        in_specs=[
            pl.BlockSpec((MLP_TM, D_MODEL), lambda i, f: (i, 0)),
            pl.BlockSpec((1, D_MODEL), lambda i, f: (0, 0)),
            pl.BlockSpec((D_MODEL, MLP_TF), lambda i, f: (0, f)),
            pl.BlockSpec((MLP_TF, D_MODEL), lambda i, f: (f, 0)),
        ],
        out_specs=pl.BlockSpec((MLP_TM, D_MODEL), lambda i, f: (i, 0)),
        out_shape=jax.ShapeDtypeStruct((m, D_MODEL), F32),
        scratch_shapes=[pltpu.VMEM((MLP_TM, D_MODEL), BF16)],
        compiler_params=pltpu.CompilerParams(
            dimension_semantics=("parallel", "arbitrary"), vmem_limit_bytes=VMEM_LIMIT_BYTES),
        name="mlp",
    )(x2, g, wu, wd)


def _t5_bucket(rel):
    nb = N_BUCKETS // 2
    max_exact = nb // 2
    ret = (rel > 0).astype(np.int32) * nb
    n = np.abs(rel).astype(np.int32)
    nf = np.maximum(n, max_exact).astype(np.float32)
    large = max_exact + (np.log(nf / max_exact) / np.log(MAX_DISTANCE / max_exact) * (nb - max_exact)).astype(np.int32)
    large = np.minimum(large, nb - 1)
    return ret + np.where(n < max_exact, n, large)


def _select_rows(onehot, table):
    return jnp.einsum("ij,j...->i...", jnp.asarray(onehot, F32), table.astype(F32), precision=lax.Precision.HIGHEST)


def _window_bias_table(t5_bias):
    n_off = 4 * BLK
    offsets = np.arange(n_off - 1) - (2 * BLK - 1)
    onehot = _t5_bucket(offsets)[:, None] == np.arange(N_BUCKETS)[None, :]
    per_offset = jnp.where((np.abs(offsets) <= WINDOW)[:, None], _select_rows(onehot, t5_bias), NEG)
    y = jnp.pad(per_offset.T, ((0, 0), (0, 1)))
    skew = jnp.tile(y, (1, BLK + 1))[:, :BLK * (n_off + 1)].reshape(A_Q_HEADS, BLK, n_off + 1)
    table = skew[:, ::-1, :3 * BLK].reshape(A_KV_HEADS, A_GROUP, BLK, 3 * BLK)
    return jnp.transpose(table, (0, 3, 1, 2)).reshape(A_KV_HEADS, 3 * BLK, A_GROUP * BLK)


def _nbr_bias_table(rpb):
    qc = np.arange(GRID_W)[:, None]
    kc = np.arange(GRID_W)[None, :]
    start_c = np.clip(qc - NA_KW // 2, 0, GRID_W - NA_KW)
    col_mask = (kc >= start_c) & (kc < start_c + NA_KW)
    dcc = np.clip(kc - qc, -(NA_KW - 1), NA_KW - 1) + (NA_KW - 1)
    onehot = dcc.reshape(-1, 1) == np.arange(2 * NA_KW - 1)[None, :]
    by_col = _select_rows(onehot, jnp.transpose(rpb, (2, 0, 1)))
    by_col = jnp.where(col_mask.reshape(-1, 1, 1), by_col, NEG).reshape(GRID_W, GRID_W, B_HEADS, 2 * NA_KH - 1)
    by_col = jnp.transpose(by_col, (2, 0, 3, 1))
    variants = [by_col[:, :, NA_KH - 1 - d:2 * NA_KH - 1 - d, :] for d in range(NA_KH)]
    return jnp.stack(variants, axis=1).reshape(B_HEADS, NA_KH, GRID_W, NA_KH * GRID_W)


def _prepare(norm_mix_g, w_in, q_norm_a, k_norm_a, t5_bias, sink_a, q_norm_b, k_norm_b, rpb_b,
             w_br_a, w_br_b, w_out, norm_mlp_g, w_up, w_down):
    scale = HEAD_DIM ** -0.5
    colgain = jnp.concatenate([
        jnp.tile(q_norm_a[0].astype(F32), A_Q_HEADS) * (scale * LOG2E),
        jnp.tile(k_norm_a[0].astype(F32), A_KV_HEADS),
        jnp.ones((A_KV_W,), F32),
        jnp.tile(q_norm_b[0].astype(F32), B_HEADS) * scale,
        jnp.tile(k_norm_b[0].astype(F32), B_HEADS),
        jnp.ones((B_W + 2 * D_MODEL,), F32)]).reshape(1, IN_WIDTH)
    return dict(
        g_mix=norm_mix_g[0].astype(F32).reshape(1, D_MODEL),
        w_in=w_in[0].astype(BF16),
        colgain=colgain,
        sink=sink_a[0].astype(F32) * LOG2E,
        bias_a=_window_bias_table(t5_bias * LOG2E),
        bias_b=_nbr_bias_table(rpb_b[0]),
        w_a=w_br_a[0].astype(BF16),
        w_b=w_br_b[0].astype(BF16),
        w_o=w_out[0].astype(BF16),
        g_mlp=norm_mlp_g[0].astype(F32).reshape(1, D_MODEL),
        w_up=w_up[0].astype(BF16),
        w_down=w_down[0].astype(BF16),
    )


def _layer(x, p):
    b, seq, _ = x.shape
    x2 = x.reshape(b * seq, D_MODEL)
    proj, qv_b, kt_b = _in_proj(x2, p["g_mix"], p["w_in"], p["colgain"])
    proj3 = proj.reshape(b, seq, PROJ_W)
    oat = _win_attn(proj3, p["sink"], p["bias_a"])
    ob = _nbr_attn(qv_b, kt_b, b, seq, p["bias_b"])
    x1 = _merge(oat, ob, proj, x2, p["w_a"], p["w_b"], p["w_o"])
    y = _mlp(x1, p["g_mlp"], p["w_up"], p["w_down"])
    return y.reshape(b, seq, D_MODEL)


def kernel(x_prompt, x_sample, norm_mix_g, w_in, q_norm_a, k_norm_a, t5_bias, sink_a, q_norm_b, k_norm_b, rpb_b,
           w_br_a, w_br_b, w_out, norm_mlp_g, w_up, w_down):
    p = _prepare(norm_mix_g, w_in, q_norm_a, k_norm_a, t5_bias, sink_a, q_norm_b, k_norm_b, rpb_b,
                 w_br_a, w_br_b, w_out, norm_mlp_g, w_up, w_down)
    return (_layer(x_prompt, p), _layer(x_sample, p))
```

```python
import functools

import jax
import jax.numpy as jnp
import numpy as np
from jax import lax
from jax.experimental import pallas as pl
from jax.experimental.pallas import tpu as pltpu

D_MODEL = 2048
HEAD_DIM = 128
A_Q_HEADS = 8
A_KV_HEADS = 2
A_GROUP = A_Q_HEADS // A_KV_HEADS
WINDOW = 128
BLK = WINDOW
N_BUCKETS = 32
MAX_DISTANCE = 128
B_HEADS = 8
GRID_W = 64
NA_KH = 8
NA_KW = 16
D_FF = 4 * D_MODEL
EPS = 1e-6
NEG = -1e30
LOG2E = float(np.log2(np.e))

A_Q_W = A_Q_HEADS * HEAD_DIM
A_KV_W = A_KV_HEADS * HEAD_DIM
B_W = B_HEADS * HEAD_DIM
IN_WIDTH = A_Q_W + 2 * A_KV_W + 3 * B_W + 2 * D_MODEL

SRC_QA = 0
SRC_KVA = SRC_QA + A_Q_W
SRC_QB = SRC_KVA + 2 * A_KV_W
SRC_KB = SRC_QB + B_W
SRC_VB = SRC_KB + B_W
SRC_GA = SRC_VB + B_W
SRC_GB = SRC_GA + D_MODEL
assert SRC_GB + D_MODEL == IN_WIDTH

COL_GA = 0
COL_GB = COL_GA + D_MODEL
COL_KA = COL_GB + D_MODEL
PROJ_W = COL_KA + A_KV_W

V7X_VMEM_BYTES = 64 * 1024 * 1024
VMEM_LIMIT_BYTES = V7X_VMEM_BYTES - 8 * 1024 * 1024

PROJ_TM = 256
PROJ_TN = 512
WIN_TQ = 4 * BLK
WIN_LOOKAHEAD = 1
NBR_LOOKAHEAD = 5
MERGE_TM = 512
MERGE_SUB = 256
MLP_TM = 1024
MLP_TF = 512

BF16 = jnp.bfloat16
F32 = jnp.float32


def _rms_scale(x):
    return lax.rsqrt(jnp.mean(x * x, axis=-1, keepdims=True) + EPS)


def _issue_ahead(n, depth, first_stage, second_stage):
    pending = [first_stage(i) for i in range(min(depth, n))]
    for i in range(n):
        if i + depth < n:
            pending.append(first_stage(i + depth))
        second_stage(i, pending.pop(0))


def _head_rmsnorm(a):
    parts = []
    for c in range(a.shape[1] // HEAD_DIM):
        blk = a[:, c * HEAD_DIM:(c + 1) * HEAD_DIM]
        parts.append(blk * _rms_scale(blk))
    return jnp.concatenate(parts, axis=1)


def _in_proj_kernel(x_ref, g_ref, w_ref, cg_ref, o_ref, qat_ref, vat_ref, qvb_ref, kbt_ref, h_ref):
    x = x_ref[...]
    h_ref[...] = (x * _rms_scale(x) * g_ref[...]).astype(BF16)
    for c0 in range(0, IN_WIDTH, PROJ_TN):
        cols = slice(c0, c0 + PROJ_TN)
        acc = jnp.dot(h_ref[...], w_ref[:, cols], preferred_element_type=F32)
        if c0 < SRC_KVA or SRC_QB <= c0 < SRC_VB:
            acc = _head_rmsnorm(acc) * cg_ref[:, cols]
        elif c0 == SRC_KVA:
            k = _head_rmsnorm(acc[:, :A_KV_W]) * cg_ref[:, c0:c0 + A_KV_W]
            acc = jnp.concatenate([k, acc[:, A_KV_W:]], axis=1)
        heads = [acc[:, j * HEAD_DIM:(j + 1) * HEAD_DIM] for j in range(PROJ_TN // HEAD_DIM)]
        if c0 < SRC_KVA:
            for j, head in enumerate(heads):
                r0 = c0 - SRC_QA + j * HEAD_DIM
                qat_ref[r0:r0 + HEAD_DIM, :] = head.T.astype(BF16)
        elif c0 == SRC_KVA:
            o_ref[:, COL_KA:COL_KA + A_KV_W] = acc[:, :A_KV_W].astype(BF16)
            for j, head in enumerate(heads[A_KV_HEADS:]):
                vat_ref[j * HEAD_DIM:(j + 1) * HEAD_DIM, :] = head.T.astype(BF16)
        elif c0 < SRC_KB:
            for j, head in enumerate(heads):
                qvb_ref[(c0 - SRC_QB) // HEAD_DIM + j] = head.astype(BF16)
        elif c0 < SRC_VB:
            for j, head in enumerate(heads):
                kbt_ref[(c0 - SRC_KB) // HEAD_DIM + j] = head.T.astype(BF16)
        elif c0 < SRC_GA:
            for j, head in enumerate(heads):
                qvb_ref[B_HEADS + (c0 - SRC_VB) // HEAD_DIM + j] = head.astype(BF16)
        else:
            d0 = COL_GA + c0 - SRC_GA if c0 < SRC_GB else COL_GB + c0 - SRC_GB
            o_ref[:, d0:d0 + PROJ_TN] = acc.astype(BF16)


def _in_proj(x2, g, w, colgain):
    m = x2.shape[0]
    resident = functools.partial(pl.BlockSpec, index_map=lambda i: (0, 0), pipeline_mode=pl.Buffered(1))
    return pl.pallas_call(
        _in_proj_kernel,
        grid=(m // PROJ_TM,),
        in_specs=[
            pl.BlockSpec((PROJ_TM, D_MODEL), lambda i: (i, 0)),
            resident((1, D_MODEL)),
            resident((D_MODEL, IN_WIDTH)),
            resident((1, IN_WIDTH)),
        ],
        out_specs=[
            pl.BlockSpec((PROJ_TM, PROJ_W), lambda i: (i, 0)),
            pl.BlockSpec((A_Q_W, PROJ_TM), lambda i: (0, i)),
            pl.BlockSpec((A_KV_W, PROJ_TM), lambda i: (0, i)),
            pl.BlockSpec((2 * B_HEADS, PROJ_TM, HEAD_DIM), lambda i: (0, i, 0)),
            pl.BlockSpec((B_HEADS, HEAD_DIM, PROJ_TM), lambda i: (0, 0, i)),
        ],
        out_shape=[
            jax.ShapeDtypeStruct((m, PROJ_W), BF16),
            jax.ShapeDtypeStruct((A_Q_W, m), BF16),
            jax.ShapeDtypeStruct((A_KV_W, m), BF16),
            jax.ShapeDtypeStruct((2 * B_HEADS, m, HEAD_DIM), BF16),
            jax.ShapeDtypeStruct((B_HEADS, HEAD_DIM, m), BF16),
        ],
        scratch_shapes=[pltpu.VMEM((PROJ_TM, D_MODEL), BF16)],
        compiler_params=pltpu.CompilerParams(
            dimension_semantics=("parallel",), vmem_limit_bytes=VMEM_LIMIT_BYTES),
        name="in_proj",
    )(x2, g, w, colgain)


def _win_attn_kernel(sink_ref, qt_ref, kp_ref, km_ref, kn_ref, vtp_ref, vtm_ref, vtn_ref, bias_ref, o_ref, *, n_tiles):
    i = pl.program_id(1)
    k_all = jnp.concatenate([kp_ref[0], km_ref[0], kn_ref[0]], axis=0)
    vt_all = jnp.concatenate([vtp_ref[...], vtm_ref[...], vtn_ref[...]], axis=1)
    key = lax.broadcasted_iota(jnp.int32, (3 * BLK, 1), 0)
    first_key = jnp.where(i == 0, BLK, 0)
    end_key = jnp.where(i == n_tiles - 1, 2 * BLK, 3 * BLK)
    n_blk = WIN_TQ // BLK
    units = [(t, kvh) for t in range(n_blk) for kvh in range(A_KV_HEADS)]

    def scores(t, kvh):
        k = k_all[t * BLK:(t + 3) * BLK, kvh * HEAD_DIM:(kvh + 1) * HEAD_DIM]
        qt = jnp.concatenate(
            [qt_ref[(kvh * A_GROUP + g) * HEAD_DIM:(kvh * A_GROUP + g + 1) * HEAD_DIM, t * BLK:(t + 1) * BLK]
             for g in range(A_GROUP)], axis=1)
        s = jnp.dot(k, qt, preferred_element_type=F32) + bias_ref[kvh]
        if t == 0:
            s = jnp.where(key >= first_key, s, NEG)
        if t == n_blk - 1:
            s = jnp.where(key < end_key, s, NEG)
        return s, jnp.max(s, axis=0, keepdims=True)

    def finish(t, kvh, s_and_max):
        s, col_max = s_and_max
        heads = [kvh * A_GROUP + g for g in range(A_GROUP)]
        vt = vt_all[kvh * HEAD_DIM:(kvh + 1) * HEAD_DIM, t * BLK:(t + 3) * BLK]
        sink = jnp.concatenate([jnp.full((1, BLK), sink_ref[h], F32) for h in heads], axis=1)
        m = jnp.maximum(col_max, sink)
        p = jnp.exp2(s - m)
        denom = jnp.sum(p, axis=0, keepdims=True) + jnp.exp2(sink - m)
        o = jnp.dot(vt, p.astype(BF16), preferred_element_type=F32) / denom
        for g, h in enumerate(heads):
            o_ref[0, h * HEAD_DIM:(h + 1) * HEAD_DIM, t * BLK:(t + 1) * BLK] = (
                o[:, g * BLK:(g + 1) * BLK].astype(BF16))

    _issue_ahead(len(units), WIN_LOOKAHEAD, lambda u: scores(*units[u]), lambda u, s: finish(*units[u], s))


def _win_attn(proj3, qat, vat, sink, bias_a):
    b, seq, _ = proj3.shape
    n_tiles = seq // WIN_TQ
    blk_per_tile = WIN_TQ // BLK
    n_blk = seq // BLK
    ka_blk = COL_KA // A_KV_W

    def prev_blk(i):
        return jnp.maximum(i * blk_per_tile - 1, 0)

    def next_blk(i):
        return jnp.minimum((i + 1) * blk_per_tile, n_blk - 1)

    return pl.pallas_call(
        functools.partial(_win_attn_kernel, n_tiles=n_tiles),
        grid=(b, n_tiles),
        in_specs=[
            pl.BlockSpec(memory_space=pltpu.SMEM),
            pl.BlockSpec((A_Q_W, WIN_TQ), lambda bi, i: (0, bi * n_tiles + i)),
            pl.BlockSpec((1, BLK, A_KV_W), lambda bi, i: (bi, prev_blk(i), ka_blk)),
            pl.BlockSpec((1, WIN_TQ, A_KV_W), lambda bi, i: (bi, i, ka_blk)),
            pl.BlockSpec((1, BLK, A_KV_W), lambda bi, i: (bi, next_blk(i), ka_blk)),
            pl.BlockSpec((A_KV_W, BLK), lambda bi, i: (0, bi * n_blk + prev_blk(i))),
            pl.BlockSpec((A_KV_W, WIN_TQ), lambda bi, i: (0, bi * n_tiles + i)),
            pl.BlockSpec((A_KV_W, BLK), lambda bi, i: (0, bi * n_blk + next_blk(i))),
            pl.BlockSpec((A_KV_HEADS, 3 * BLK, A_GROUP * BLK), lambda bi, i: (0, 0, 0)),
        ],
        out_specs=pl.BlockSpec((1, A_Q_W, WIN_TQ), lambda bi, i: (bi, 0, i)),
        out_shape=jax.ShapeDtypeStruct((b, A_Q_W, seq), BF16),
        compiler_params=pltpu.CompilerParams(
            dimension_semantics=("parallel", "arbitrary"), vmem_limit_bytes=VMEM_LIMIT_BYTES),
        name="win_attn",
    )(sink, qat, proj3, proj3, proj3, vat, vat, vat, bias_a)


def _nbr_attn_kernel(q_ref, kt_ref, v_ref, bias_ref, o_ref, *, rows):
    def first_key_row(r):
        return min(max(r - NA_KH // 2, 0), rows - NA_KH)

    kt_even = kt_ref[0]
    kt_odd = jnp.concatenate([kt_even[:, GRID_W:], kt_even[:, :GRID_W]], axis=1)

    def scores(r):
        rs = first_key_row(r)
        k0 = (rs - rs % 2) * GRID_W
        kt = (kt_odd if rs % 2 else kt_even)[:, k0:k0 + NA_KH * GRID_W]
        q = q_ref[0, r * GRID_W:(r + 1) * GRID_W, :]
        s = jnp.dot(q, kt, preferred_element_type=F32) + bias_ref[0, r - rs]
        return s, jnp.max(s, axis=-1, keepdims=True)

    def finish(r, s_and_max):
        s, m = s_and_max
        rs = first_key_row(r)
        v = v_ref[0, rs * GRID_W:(rs + NA_KH) * GRID_W, :]
        p = jnp.exp(s - m)
        denom = jnp.sum(p, axis=-1, keepdims=True)
        o = jnp.dot(p.astype(BF16), v, preferred_element_type=F32) / denom
        o_ref[0, r * GRID_W:(r + 1) * GRID_W, :] = o.astype(BF16)

    _issue_ahead(rows, NBR_LOOKAHEAD, scores, finish)


def _nbr_attn(qv_b, kt_b, b, seq, bias_b):
    rows = seq // GRID_W
    return pl.pallas_call(
        functools.partial(_nbr_attn_kernel, rows=rows),
        grid=(b, B_HEADS),
        in_specs=[
            pl.BlockSpec((1, seq, HEAD_DIM), lambda bi, h: (h, bi, 0)),
            pl.BlockSpec((1, HEAD_DIM, seq), lambda bi, h: (h, 0, bi)),
            pl.BlockSpec((1, seq, HEAD_DIM), lambda bi, h: (B_HEADS + h, bi, 0)),
            pl.BlockSpec((1, NA_KH, GRID_W, NA_KH * GRID_W), lambda bi, h: (h, 0, 0, 0)),
        ],
        out_specs=pl.BlockSpec((1, seq, HEAD_DIM), lambda bi, h: (h, bi, 0)),
        out_shape=jax.ShapeDtypeStruct((B_HEADS, b * seq, HEAD_DIM), BF16),
        compiler_params=pltpu.CompilerParams(
            dimension_semantics=("parallel", "arbitrary"), vmem_limit_bytes=VMEM_LIMIT_BYTES),
        name="nbr_attn",
    )(qv_b, kt_b, qv_b, bias_b)


def _merge_kernel(oat_ref, ob_ref, ga_ref, gb_ref, x_ref, wa_ref, wb_ref, wo_ref, o_ref):
    merged = []
    for r0 in range(0, MERGE_TM, MERGE_SUB):
        rows = slice(r0, r0 + MERGE_SUB)
        ya = lax.dot_general(oat_ref[0, :, rows], wa_ref[...], (((0,), (0,)), ((), ())), preferred_element_type=F32)
        ob = jnp.concatenate([ob_ref[h, rows, :] for h in range(B_HEADS)], axis=1)
        yb = jnp.dot(ob, wb_ref[...], preferred_element_type=F32)
        gated = (jax.nn.sigmoid(ga_ref[rows, :].astype(F32)) * ya
                 + jax.nn.sigmoid(gb_ref[rows, :].astype(F32)) * yb)
        merged.append(gated.astype(BF16))
    for r0, mrg in zip(range(0, MERGE_TM, MERGE_SUB), merged):
        rows = slice(r0, r0 + MERGE_SUB)
        o_ref[rows, :] = x_ref[rows, :] + jnp.dot(mrg, wo_ref[...], preferred_element_type=F32)


def _merge(oat, ob, proj, x2, wa, wb, wo):
    m = x2.shape[0]
    tiles_per_seq = oat.shape[2] // MERGE_TM
    resident = functools.partial(pl.BlockSpec, index_map=lambda i: (0, 0), pipeline_mode=pl.Buffered(1))
    return pl.pallas_call(
        _merge_kernel,
        grid=(m // MERGE_TM,),
        in_specs=[
            pl.BlockSpec((1, A_Q_W, MERGE_TM), lambda i: (i // tiles_per_seq, 0, i % tiles_per_seq)),
            pl.BlockSpec((B_HEADS, MERGE_TM, HEAD_DIM), lambda i: (0, i, 0)),
            pl.BlockSpec((MERGE_TM, D_MODEL), lambda i: (i, COL_GA // D_MODEL)),
            pl.BlockSpec((MERGE_TM, D_MODEL), lambda i: (i, COL_GB // D_MODEL)),
            pl.BlockSpec((MERGE_TM, D_MODEL), lambda i: (i, 0)),
            resident((A_Q_W, D_MODEL)),
            resident((B_W, D_MODEL)),
            resident((D_MODEL, D_MODEL)),
        ],
        out_specs=pl.BlockSpec((MERGE_TM, D_MODEL), lambda i: (i, 0)),
        out_shape=jax.ShapeDtypeStruct((m, D_MODEL), F32),
        compiler_params=pltpu.CompilerParams(
            dimension_semantics=("parallel",), vmem_limit_bytes=VMEM_LIMIT_BYTES),
        name="merge",
    )(oat, ob, proj, proj, x2, wa, wb, wo)


MLP_NF = D_FF // MLP_TF


def _mlp_kernel(x_ref, g_ref, wu_ref, wd_ref, o_ref, h_ref):
    i = pl.program_id(0)
    f = pl.program_id(1)

    def norm():
        x = x_ref[...]
        h_ref[...] = (x * _rms_scale(x) * g_ref[...]).astype(BF16)

    def down_of_up():
        u = jnp.dot(h_ref[...], wu_ref[...], preferred_element_type=F32)
        a = jnp.square(jnp.maximum(u, 0.0)).astype(BF16)
        return jnp.dot(a, wd_ref[...], preferred_element_type=F32)

    @pl.when(jnp.logical_and(i == 0, f == 0))
    def _():
        norm()

    @pl.when(f == 0)
    def _():
        o_ref[...] = x_ref[...] + down_of_up()

    @pl.when(jnp.logical_and(f > 0, f < MLP_NF - 1))
    def _():
        o_ref[...] += down_of_up()

    @pl.when(f == MLP_NF - 1)
    def _():
        o_ref[...] += down_of_up()
        norm()


def _mlp(x2, g, wu, wd):
    m = x2.shape[0]
    n_tiles = m // MLP_TM
    return pl.pallas_call(
        _mlp_kernel,
        grid=(n_tiles, MLP_NF),
        in_specs=[
            pl.BlockSpec((MLP_TM, D_MODEL),
                         lambda i, f: (jnp.minimum(i + (f == MLP_NF - 1).astype(jnp.int32), n_tiles - 1), 0)),
            pl.BlockSpec((1, D_MODEL), lambda i, f: (0, 0)),
            pl.BlockSpec((D_MODEL, MLP_TF), lambda i, f: (0, f)),
            pl.BlockSpec((MLP_TF, D_MODEL), lambda i, f: (f, 0)),
        ],
        out_specs=pl.BlockSpec((MLP_TM, D_MODEL), lambda i, f: (i, 0)),
        out_shape=jax.ShapeDtypeStruct((m, D_MODEL), F32),
        scratch_shapes=[pltpu.VMEM((MLP_TM, D_MODEL), BF16)],
        compiler_params=pltpu.CompilerParams(
            dimension_semantics=("arbitrary", "arbitrary"), vmem_limit_bytes=VMEM_LIMIT_BYTES),
        name="mlp",
    )(x2, g, wu, wd)


def _t5_bucket(rel):
    nb = N_BUCKETS // 2
    max_exact = nb // 2
    ret = (rel > 0).astype(np.int32) * nb
    n = np.abs(rel).astype(np.int32)
    nf = np.maximum(n, max_exact).astype(np.float32)
    large = max_exact + (np.log(nf / max_exact) / np.log(MAX_DISTANCE / max_exact) * (nb - max_exact)).astype(np.int32)
    large = np.minimum(large, nb - 1)
    return ret + np.where(n < max_exact, n, large)


def _select_rows(onehot, table):
    return jnp.einsum("ij,j...->i...", jnp.asarray(onehot, F32), table.astype(F32), precision=lax.Precision.HIGHEST)


def _window_bias_table(t5_bias):
    n_off = 4 * BLK
    offsets = np.arange(n_off - 1) - (2 * BLK - 1)
    onehot = _t5_bucket(offsets)[:, None] == np.arange(N_BUCKETS)[None, :]
    per_offset = jnp.where((np.abs(offsets) <= WINDOW)[:, None], _select_rows(onehot, t5_bias), NEG)
    y = jnp.pad(per_offset.T, ((0, 0), (0, 1)))
    skew = jnp.tile(y, (1, BLK + 1))[:, :BLK * (n_off + 1)].reshape(A_Q_HEADS, BLK, n_off + 1)
    table = skew[:, ::-1, :3 * BLK].reshape(A_KV_HEADS, A_GROUP, BLK, 3 * BLK)
    return jnp.transpose(table, (0, 3, 1, 2)).reshape(A_KV_HEADS, 3 * BLK, A_GROUP * BLK)


def _nbr_bias_table(rpb):
    qc = np.arange(GRID_W)[:, None]
    kc = np.arange(GRID_W)[None, :]
    start_c = np.clip(qc - NA_KW // 2, 0, GRID_W - NA_KW)
    col_mask = (kc >= start_c) & (kc < start_c + NA_KW)
    dcc = np.clip(kc - qc, -(NA_KW - 1), NA_KW - 1) + (NA_KW - 1)
    onehot = dcc.reshape(-1, 1) == np.arange(2 * NA_KW - 1)[None, :]
    by_col = _select_rows(onehot, jnp.transpose(rpb, (2, 0, 1)))
    by_col = jnp.where(col_mask.reshape(-1, 1, 1), by_col, NEG).reshape(GRID_W, GRID_W, B_HEADS, 2 * NA_KH - 1)
    by_col = jnp.transpose(by_col, (2, 0, 3, 1))
    variants = [by_col[:, :, NA_KH - 1 - d:2 * NA_KH - 1 - d, :] for d in range(NA_KH)]
    return jnp.stack(variants, axis=1).reshape(B_HEADS, NA_KH, GRID_W, NA_KH * GRID_W)


def _prepare(norm_mix_g, w_in, q_norm_a, k_norm_a, t5_bias, sink_a, q_norm_b, k_norm_b, rpb_b,
             w_br_a, w_br_b, w_out, norm_mlp_g, w_up, w_down):
    scale = HEAD_DIM ** -0.5
    colgain = jnp.concatenate([
        jnp.tile(q_norm_a[0].astype(F32), A_Q_HEADS) * (scale * LOG2E),
        jnp.tile(k_norm_a[0].astype(F32), A_KV_HEADS),
        jnp.ones((A_KV_W,), F32),
        jnp.tile(q_norm_b[0].astype(F32), B_HEADS) * scale,
        jnp.tile(k_norm_b[0].astype(F32), B_HEADS),
        jnp.ones((B_W + 2 * D_MODEL,), F32)]).reshape(1, IN_WIDTH)
    return dict(
        g_mix=norm_mix_g[0].astype(F32).reshape(1, D_MODEL),
        w_in=w_in[0].astype(BF16),
        colgain=colgain,
        sink=sink_a[0].astype(F32) * LOG2E,
        bias_a=_window_bias_table(t5_bias * LOG2E),
        bias_b=_nbr_bias_table(rpb_b[0]),
        w_a=w_br_a[0].astype(BF16),
        w_b=w_br_b[0].astype(BF16),
        w_o=w_out[0].astype(BF16),
        g_mlp=norm_mlp_g[0].astype(F32).reshape(1, D_MODEL),
        w_up=w_up[0].astype(BF16),
        w_down=w_down[0].astype(BF16),
    )


def _layer(x, p):
    b, seq, _ = x.shape
    x2 = x.reshape(b * seq, D_MODEL)
    proj, qat, vat, qv_b, kt_b = _in_proj(x2, p["g_mix"], p["w_in"], p["colgain"])
    proj3 = proj.reshape(b, seq, PROJ_W)
    oat = _win_attn(proj3, qat, vat, p["sink"], p["bias_a"])
    ob = _nbr_attn(qv_b, kt_b, b, seq, p["bias_b"])
    x1 = _merge(oat, ob, proj, x2, p["w_a"], p["w_b"], p["w_o"])
    y = _mlp(x1, p["g_mlp"], p["w_up"], p["w_down"])
    return y.reshape(b, seq, D_MODEL)


def kernel(x_prompt, x_sample, norm_mix_g, w_in, q_norm_a, k_norm_a, t5_bias, sink_a, q_norm_b, k_norm_b, rpb_b,
           w_br_a, w_br_b, w_out, norm_mlp_g, w_up, w_down):
    p = _prepare(norm_mix_g, w_in, q_norm_a, k_norm_a, t5_bias, sink_a, q_norm_b, k_norm_b, rpb_b,
                 w_br_a, w_br_b, w_out, norm_mlp_g, w_up, w_down)
    return (_layer(x_prompt, p), _layer(x_sample, p))
```

```python
import functools

import jax
import jax.numpy as jnp
import numpy as np
from jax import lax
from jax.experimental import pallas as pl
from jax.experimental.pallas import tpu as pltpu

D_MODEL = 2048
HEAD_DIM = 128
A_Q_HEADS = 8
A_KV_HEADS = 2
A_GROUP = A_Q_HEADS // A_KV_HEADS
WINDOW = 128
BLK = WINDOW
N_BUCKETS = 32
MAX_DISTANCE = 128
B_HEADS = 8
GRID_W = 64
NA_KH = 8
NA_KW = 16
D_FF = 4 * D_MODEL
EPS = 1e-6
NEG = -1e30
LOG2E = float(np.log2(np.e))

A_Q_W = A_Q_HEADS * HEAD_DIM
A_KV_W = A_KV_HEADS * HEAD_DIM
B_W = B_HEADS * HEAD_DIM
IN_WIDTH = A_Q_W + 2 * A_KV_W + 3 * B_W + 2 * D_MODEL

SRC_QA = 0
SRC_KVA = SRC_QA + A_Q_W
SRC_QB = SRC_KVA + 2 * A_KV_W
SRC_KB = SRC_QB + B_W
SRC_VB = SRC_KB + B_W
SRC_GA = SRC_VB + B_W
SRC_GB = SRC_GA + D_MODEL
assert SRC_GB + D_MODEL == IN_WIDTH

COL_GA = 0
COL_GB = COL_GA + D_MODEL
COL_QA = COL_GB + D_MODEL
COL_KVA = COL_QA + A_Q_W
PROJ_W = COL_KVA + 2 * A_KV_W

V7X_VMEM_BYTES = 64 * 1024 * 1024
VMEM_LIMIT_BYTES = V7X_VMEM_BYTES - 8 * 1024 * 1024

PROJ_TM = 256
PROJ_TN = 512
WIN_TQ = 4 * BLK
WIN_LOOKAHEAD = 1
NBR_LOOKAHEAD = 5
MERGE_TM = 512
MERGE_SUB = 256
MLP_TM = 1024
MLP_TF = 1024

BF16 = jnp.bfloat16
F32 = jnp.float32


def _rms_scale(x):
    return lax.rsqrt(jnp.mean(x * x, axis=-1, keepdims=True) + EPS)


def _issue_ahead(n, depth, first_stage, second_stage):
    pending = [first_stage(i) for i in range(min(depth, n))]
    for i in range(n):
        if i + depth < n:
            pending.append(first_stage(i + depth))
        second_stage(i, pending.pop(0))


def _head_rmsnorm(a):
    parts = []
    for c in range(a.shape[1] // HEAD_DIM):
        blk = a[:, c * HEAD_DIM:(c + 1) * HEAD_DIM]
        parts.append(blk * _rms_scale(blk))
    return jnp.concatenate(parts, axis=1)


def _in_proj_kernel(x_ref, g_ref, w_ref, cg_ref, o_ref, qvb_ref, kbt_ref, h_ref):
    x = x_ref[...]
    h_ref[...] = (x * _rms_scale(x) * g_ref[...]).astype(BF16)
    for c0 in range(0, IN_WIDTH, PROJ_TN):
        cols = slice(c0, c0 + PROJ_TN)
        acc = jnp.dot(h_ref[...], w_ref[:, cols], preferred_element_type=F32)
        if c0 < SRC_KVA or SRC_QB <= c0 < SRC_VB:
            acc = _head_rmsnorm(acc) * cg_ref[:, cols]
        elif c0 == SRC_KVA:
            k = _head_rmsnorm(acc[:, :A_KV_W]) * cg_ref[:, c0:c0 + A_KV_W]
            acc = jnp.concatenate([k, acc[:, A_KV_W:]], axis=1)
        if SRC_QB <= c0 < SRC_GA:
            for j in range(PROJ_TN // HEAD_DIM):
                head = acc[:, j * HEAD_DIM:(j + 1) * HEAD_DIM]
                if c0 < SRC_KB:
                    qvb_ref[(c0 - SRC_QB) // HEAD_DIM + j] = head.astype(BF16)
                elif c0 < SRC_VB:
                    kbt_ref[(c0 - SRC_KB) // HEAD_DIM + j] = head.T.astype(BF16)
                else:
                    qvb_ref[B_HEADS + (c0 - SRC_VB) // HEAD_DIM + j] = head.astype(BF16)
        else:
            if c0 < SRC_KVA:
                d0 = COL_QA + c0 - SRC_QA
            elif c0 == SRC_KVA:
                d0 = COL_KVA
            elif c0 < SRC_GB:
                d0 = COL_GA + c0 - SRC_GA
            else:
                d0 = COL_GB + c0 - SRC_GB
            o_ref[:, d0:d0 + PROJ_TN] = acc.astype(BF16)


def _in_proj(x2, g, w, colgain):
    m = x2.shape[0]
    resident = functools.partial(pl.BlockSpec, index_map=lambda i: (0, 0), pipeline_mode=pl.Buffered(1))
    return pl.pallas_call(
        _in_proj_kernel,
        grid=(m // PROJ_TM,),
        in_specs=[
            pl.BlockSpec((PROJ_TM, D_MODEL), lambda i: (i, 0)),
            resident((1, D_MODEL)),
            resident((D_MODEL, IN_WIDTH)),
            resident((1, IN_WIDTH)),
        ],
        out_specs=[
            pl.BlockSpec((PROJ_TM, PROJ_W), lambda i: (i, 0)),
            pl.BlockSpec((2 * B_HEADS, PROJ_TM, HEAD_DIM), lambda i: (0, i, 0)),
            pl.BlockSpec((B_HEADS, HEAD_DIM, PROJ_TM), lambda i: (0, 0, i)),
        ],
        out_shape=[
            jax.ShapeDtypeStruct((m, PROJ_W), BF16),
            jax.ShapeDtypeStruct((2 * B_HEADS, m, HEAD_DIM), BF16),
            jax.ShapeDtypeStruct((B_HEADS, HEAD_DIM, m), BF16),
        ],
        scratch_shapes=[pltpu.VMEM((PROJ_TM, D_MODEL), BF16)],
        compiler_params=pltpu.CompilerParams(
            dimension_semantics=("parallel",), vmem_limit_bytes=VMEM_LIMIT_BYTES),
        name="in_proj",
    )(x2, g, w, colgain)


def _win_attn_kernel(sink_ref, q_ref, kvp_ref, kvm_ref, kvn_ref, bias_ref, o_ref, *, n_tiles):
    i = pl.program_id(1)
    kv = jnp.concatenate([kvp_ref[0], kvm_ref[0], kvn_ref[0]], axis=0)
    key = lax.broadcasted_iota(jnp.int32, (3 * BLK, 1), 0)
    first_key = jnp.where(i == 0, BLK, 0)
    end_key = jnp.where(i == n_tiles - 1, 2 * BLK, 3 * BLK)
    n_blk = WIN_TQ // BLK
    units = [(t, kvh) for t in range(n_blk) for kvh in range(A_KV_HEADS)]

    def scores(t, kvh):
        k = kv[t * BLK:(t + 3) * BLK, kvh * HEAD_DIM:(kvh + 1) * HEAD_DIM]
        q = jnp.concatenate(
            [q_ref[0, t * BLK:(t + 1) * BLK, (kvh * A_GROUP + g) * HEAD_DIM:(kvh * A_GROUP + g + 1) * HEAD_DIM]
             for g in range(A_GROUP)], axis=0)
        s = lax.dot_general(k, q, (((1,), (1,)), ((), ())), preferred_element_type=F32)
        s = s + bias_ref[kvh]
        if t == 0:
            s = jnp.where(key >= first_key, s, NEG)
        if t == n_blk - 1:
            s = jnp.where(key < end_key, s, NEG)
        return s, jnp.max(s, axis=0, keepdims=True)

    def finish(t, kvh, s_and_max):
        s, col_max = s_and_max
        heads = [kvh * A_GROUP + g for g in range(A_GROUP)]
        v = kv[t * BLK:(t + 3) * BLK, A_KV_W + kvh * HEAD_DIM:A_KV_W + (kvh + 1) * HEAD_DIM]
        sink = jnp.concatenate([jnp.full((1, BLK), sink_ref[h], F32) for h in heads], axis=1)
        m = jnp.maximum(col_max, sink)
        p = jnp.exp2(s - m)
        denom = jnp.sum(p, axis=0, keepdims=True) + jnp.exp2(sink - m)
        o = lax.dot_general(v, p.astype(BF16), (((0,), (0,)), ((), ())), preferred_element_type=F32) / denom
        for g, h in enumerate(heads):
            o_ref[0, h * HEAD_DIM:(h + 1) * HEAD_DIM, t * BLK:(t + 1) * BLK] = (
                o[:, g * BLK:(g + 1) * BLK].astype(BF16))

    _issue_ahead(len(units), WIN_LOOKAHEAD, lambda u: scores(*units[u]), lambda u, s: finish(*units[u], s))


def _win_attn(proj3, sink, bias_a):
    b, seq, _ = proj3.shape
    n_tiles = seq // WIN_TQ
    blk_per_tile = WIN_TQ // BLK
    last_blk = seq // BLK - 1
    kva_wide = COL_KVA // (2 * A_KV_W)
    return pl.pallas_call(
        functools.partial(_win_attn_kernel, n_tiles=n_tiles),
        grid=(b, n_tiles),
        in_specs=[
            pl.BlockSpec(memory_space=pltpu.SMEM),
            pl.BlockSpec((1, WIN_TQ, A_Q_W), lambda bi, i: (bi, i, COL_QA // A_Q_W)),
            pl.BlockSpec((1, BLK, 2 * A_KV_W),
                         lambda bi, i: (bi, jnp.maximum(i * blk_per_tile - 1, 0), kva_wide)),
            pl.BlockSpec((1, WIN_TQ, 2 * A_KV_W), lambda bi, i: (bi, i, kva_wide)),
            pl.BlockSpec((1, BLK, 2 * A_KV_W),
                         lambda bi, i: (bi, jnp.minimum((i + 1) * blk_per_tile, last_blk), kva_wide)),
            pl.BlockSpec((A_KV_HEADS, 3 * BLK, A_GROUP * BLK), lambda bi, i: (0, 0, 0)),
        ],
        out_specs=pl.BlockSpec((1, A_Q_W, WIN_TQ), lambda bi, i: (bi, 0, i)),
        out_shape=jax.ShapeDtypeStruct((b, A_Q_W, seq), BF16),
        compiler_params=pltpu.CompilerParams(
            dimension_semantics=("parallel", "arbitrary"), vmem_limit_bytes=VMEM_LIMIT_BYTES),
        name="win_attn",
    )(sink, proj3, proj3, proj3, proj3, bias_a)


def _nbr_attn_kernel(q_ref, kt_ref, v_ref, bias_ref, o_ref, *, rows):
    def first_key_row(r):
        return min(max(r - NA_KH // 2, 0), rows - NA_KH)

    kt_even = kt_ref[0]
    kt_odd = jnp.concatenate([kt_even[:, GRID_W:], kt_even[:, :GRID_W]], axis=1)

    def scores(r):
        rs = first_key_row(r)
        k0 = (rs - rs % 2) * GRID_W
        kt = (kt_odd if rs % 2 else kt_even)[:, k0:k0 + NA_KH * GRID_W]
        q = q_ref[0, r * GRID_W:(r + 1) * GRID_W, :]
        s = jnp.dot(q, kt, preferred_element_type=F32) + bias_ref[0, r - rs]
        return s, jnp.max(s, axis=-1, keepdims=True)

    def finish(r, s_and_max):
        s, m = s_and_max
        rs = first_key_row(r)
        v = v_ref[0, rs * GRID_W:(rs + NA_KH) * GRID_W, :]
        p = jnp.exp(s - m)
        denom = jnp.sum(p, axis=-1, keepdims=True)
        o = jnp.dot(p.astype(BF16), v, preferred_element_type=F32) / denom
        o_ref[0, r * GRID_W:(r + 1) * GRID_W, :] = o.astype(BF16)

    _issue_ahead(rows, NBR_LOOKAHEAD, scores, finish)


def _nbr_attn(qv_b, kt_b, b, seq, bias_b):
    rows = seq // GRID_W
    return pl.pallas_call(
        functools.partial(_nbr_attn_kernel, rows=rows),
        grid=(b, B_HEADS),
        in_specs=[
            pl.BlockSpec((1, seq, HEAD_DIM), lambda bi, h: (h, bi, 0)),
            pl.BlockSpec((1, HEAD_DIM, seq), lambda bi, h: (h, 0, bi)),
            pl.BlockSpec((1, seq, HEAD_DIM), lambda bi, h: (B_HEADS + h, bi, 0)),
            pl.BlockSpec((1, NA_KH, GRID_W, NA_KH * GRID_W), lambda bi, h: (h, 0, 0, 0)),
        ],
        out_specs=pl.BlockSpec((1, seq, HEAD_DIM), lambda bi, h: (h, bi, 0)),
        out_shape=jax.ShapeDtypeStruct((B_HEADS, b * seq, HEAD_DIM), BF16),
        compiler_params=pltpu.CompilerParams(
            dimension_semantics=("parallel", "arbitrary"), vmem_limit_bytes=VMEM_LIMIT_BYTES),
        name="nbr_attn",
    )(qv_b, kt_b, qv_b, bias_b)


def _merge_kernel(oat_ref, ob_ref, ga_ref, gb_ref, x_ref, wa_ref, wb_ref, wo_ref, o_ref):
    merged = []
    for r0 in range(0, MERGE_TM, MERGE_SUB):
        rows = slice(r0, r0 + MERGE_SUB)
        ya = lax.dot_general(oat_ref[0, :, rows], wa_ref[...], (((0,), (0,)), ((), ())), preferred_element_type=F32)
        ob = jnp.concatenate([ob_ref[h, rows, :] for h in range(B_HEADS)], axis=1)
        yb = jnp.dot(ob, wb_ref[...], preferred_element_type=F32)
        gated = (jax.nn.sigmoid(ga_ref[rows, :].astype(F32)) * ya
                 + jax.nn.sigmoid(gb_ref[rows, :].astype(F32)) * yb)
        merged.append(gated.astype(BF16))
    for r0, mrg in zip(range(0, MERGE_TM, MERGE_SUB), merged):
        rows = slice(r0, r0 + MERGE_SUB)
        o_ref[rows, :] = (x_ref[rows, :] + jnp.dot(mrg, wo_ref[...], preferred_element_type=F32)).astype(BF16)


def _merge(oat, ob, proj, x2, wa, wb, wo):
    m = x2.shape[0]
    tiles_per_seq = oat.shape[2] // MERGE_TM
    resident = functools.partial(pl.BlockSpec, index_map=lambda i: (0, 0), pipeline_mode=pl.Buffered(1))
    return pl.pallas_call(
        _merge_kernel,
        grid=(m // MERGE_TM,),
        in_specs=[
            pl.BlockSpec((1, A_Q_W, MERGE_TM), lambda i: (i // tiles_per_seq, 0, i % tiles_per_seq)),
            pl.BlockSpec((B_HEADS, MERGE_TM, HEAD_DIM), lambda i: (0, i, 0)),
            pl.BlockSpec((MERGE_TM, D_MODEL), lambda i: (i, COL_GA // D_MODEL)),
            pl.BlockSpec((MERGE_TM, D_MODEL), lambda i: (i, COL_GB // D_MODEL)),
            pl.BlockSpec((MERGE_TM, D_MODEL), lambda i: (i, 0)),
            resident((A_Q_W, D_MODEL)),
            resident((B_W, D_MODEL)),
            resident((D_MODEL, D_MODEL)),
        ],
        out_specs=pl.BlockSpec((MERGE_TM, D_MODEL), lambda i: (i, 0)),
        out_shape=jax.ShapeDtypeStruct((m, D_MODEL), BF16),
        compiler_params=pltpu.CompilerParams(
            dimension_semantics=("parallel",), vmem_limit_bytes=VMEM_LIMIT_BYTES),
        name="merge",
    )(oat, ob, proj, proj, x2, wa, wb, wo)


def _mlp_kernel(x_ref, g_ref, wu_ref, wd_ref, o_ref, h_ref):
    @pl.when(pl.program_id(1) == 0)
    def _():
        x = x_ref[...].astype(F32)
        h_ref[...] = (x * _rms_scale(x) * g_ref[...]).astype(BF16)
        o_ref[...] = x

    u = jnp.dot(h_ref[...], wu_ref[...], preferred_element_type=F32)
    a = jnp.square(jnp.maximum(u, 0.0)).astype(BF16)
    o_ref[...] += jnp.dot(a, wd_ref[...], preferred_element_type=F32)


def _mlp(x2, g, wu, wd):
    m = x2.shape[0]
    return pl.pallas_call(
        _mlp_kernel,
        grid=(m // MLP_TM, D_FF // MLP_TF),
        in_specs=[
            pl.BlockSpec((MLP_TM, D_MODEL), lambda i, f: (i, 0)),
            pl.BlockSpec((1, D_MODEL), lambda i, f: (0, 0)),
            pl.BlockSpec((D_MODEL, MLP_TF), lambda i, f: (0, f)),
            pl.BlockSpec((MLP_TF, D_MODEL), lambda i, f: (f, 0)),
        ],
        out_specs=pl.BlockSpec((MLP_TM, D_MODEL), lambda i, f: (i, 0)),
        out_shape=jax.ShapeDtypeStruct((m, D_MODEL), F32),
        scratch_shapes=[pltpu.VMEM((MLP_TM, D_MODEL), BF16)],
        compiler_params=pltpu.CompilerParams(
            dimension_semantics=("parallel", "arbitrary"), vmem_limit_bytes=VMEM_LIMIT_BYTES),
        name="mlp",
    )(x2, g, wu, wd)


def _t5_bucket(rel):
    nb = N_BUCKETS // 2
    max_exact = nb // 2
    ret = (rel > 0).astype(np.int32) * nb
    n = np.abs(rel).astype(np.int32)
    nf = np.maximum(n, max_exact).astype(np.float32)
    large = max_exact + (np.log(nf / max_exact) / np.log(MAX_DISTANCE / max_exact) * (nb - max_exact)).astype(np.int32)
    large = np.minimum(large, nb - 1)
    return ret + np.where(n < max_exact, n, large)


def _select_rows(onehot, table):
    return jnp.einsum("ij,j...->i...", jnp.asarray(onehot, F32), table.astype(F32), precision=lax.Precision.HIGHEST)


def _window_bias_table(t5_bias):
    n_off = 4 * BLK
    offsets = np.arange(n_off - 1) - (2 * BLK - 1)
    onehot = _t5_bucket(offsets)[:, None] == np.arange(N_BUCKETS)[None, :]
    per_offset = jnp.where((np.abs(offsets) <= WINDOW)[:, None], _select_rows(onehot, t5_bias), NEG)
    y = jnp.pad(per_offset.T, ((0, 0), (0, 1)))
    skew = jnp.tile(y, (1, BLK + 1))[:, :BLK * (n_off + 1)].reshape(A_Q_HEADS, BLK, n_off + 1)
    table = skew[:, ::-1, :3 * BLK].reshape(A_KV_HEADS, A_GROUP, BLK, 3 * BLK)
    return jnp.transpose(table, (0, 3, 1, 2)).reshape(A_KV_HEADS, 3 * BLK, A_GROUP * BLK)


def _nbr_bias_table(rpb):
    qc = np.arange(GRID_W)[:, None]
    kc = np.arange(GRID_W)[None, :]
    start_c = np.clip(qc - NA_KW // 2, 0, GRID_W - NA_KW)
    col_mask = (kc >= start_c) & (kc < start_c + NA_KW)
    dcc = np.clip(kc - qc, -(NA_KW - 1), NA_KW - 1) + (NA_KW - 1)
    onehot = dcc.reshape(-1, 1) == np.arange(2 * NA_KW - 1)[None, :]
    by_col = _select_rows(onehot, jnp.transpose(rpb, (2, 0, 1)))
    by_col = jnp.where(col_mask.reshape(-1, 1, 1), by_col, NEG).reshape(GRID_W, GRID_W, B_HEADS, 2 * NA_KH - 1)
    by_col = jnp.transpose(by_col, (2, 0, 3, 1))
    variants = [by_col[:, :, NA_KH - 1 - d:2 * NA_KH - 1 - d, :] for d in range(NA_KH)]
    return jnp.stack(variants, axis=1).reshape(B_HEADS, NA_KH, GRID_W, NA_KH * GRID_W)


def _prepare(norm_mix_g, w_in, q_norm_a, k_norm_a, t5_bias, sink_a, q_norm_b, k_norm_b, rpb_b,
             w_br_a, w_br_b, w_out, norm_mlp_g, w_up, w_down):
    scale = HEAD_DIM ** -0.5
    colgain = jnp.concatenate([
        jnp.tile(q_norm_a[0].astype(F32), A_Q_HEADS) * (scale * LOG2E),
        jnp.tile(k_norm_a[0].astype(F32), A_KV_HEADS),
        jnp.ones((A_KV_W,), F32),
        jnp.tile(q_norm_b[0].astype(F32), B_HEADS) * scale,
        jnp.tile(k_norm_b[0].astype(F32), B_HEADS),
        jnp.ones((B_W + 2 * D_MODEL,), F32)]).reshape(1, IN_WIDTH)
    return dict(
        g_mix=norm_mix_g[0].astype(F32).reshape(1, D_MODEL),
        w_in=w_in[0].astype(BF16),
        colgain=colgain,
        sink=sink_a[0].astype(F32) * LOG2E,
        bias_a=_window_bias_table(t5_bias * LOG2E),
        bias_b=_nbr_bias_table(rpb_b[0]),
        w_a=w_br_a[0].astype(BF16),
        w_b=w_br_b[0].astype(BF16),
        w_o=w_out[0].astype(BF16),
        g_mlp=norm_mlp_g[0].astype(F32).reshape(1, D_MODEL),
        w_up=w_up[0].astype(BF16),
        w_down=w_down[0].astype(BF16),
    )


def _layer(x, p):
    b, seq, _ = x.shape
    x2 = x.reshape(b * seq, D_MODEL)
    proj, qv_b, kt_b = _in_proj(x2, p["g_mix"], p["w_in"], p["colgain"])
    proj3 = proj.reshape(b, seq, PROJ_W)
    oat = _win_attn(proj3, p["sink"], p["bias_a"])
    ob = _nbr_attn(qv_b, kt_b, b, seq, p["bias_b"])
    x1 = _merge(oat, ob, proj, x2, p["w_a"], p["w_b"], p["w_o"])
    y = _mlp(x1, p["g_mlp"], p["w_up"], p["w_down"])
    return y.reshape(b, seq, D_MODEL)


def kernel(x_prompt, x_sample, norm_mix_g, w_in, q_norm_a, k_norm_a, t5_bias, sink_a, q_norm_b, k_norm_b, rpb_b,
           w_br_a, w_br_b, w_out, norm_mlp_g, w_up, w_down):
    p = _prepare(norm_mix_g, w_in, q_norm_a, k_norm_a, t5_bias, sink_a, q_norm_b, k_norm_b, rpb_b,
                 w_br_a, w_br_b, w_out, norm_mlp_g, w_up, w_down)
    return (_layer(x_prompt, p), _layer(x_sample, p))
```

```python
import functools

import jax
import jax.numpy as jnp
import numpy as np
from jax import lax
from jax.experimental import pallas as pl
from jax.experimental.pallas import tpu as pltpu

D_MODEL = 2048
HEAD_DIM = 128
A_Q_HEADS = 8
A_KV_HEADS = 2
A_GROUP = A_Q_HEADS // A_KV_HEADS
WINDOW = 128
BLK = WINDOW
N_BUCKETS = 32
MAX_DISTANCE = 128
B_HEADS = 8
GRID_W = 64
NA_KH = 8
NA_KW = 16
D_FF = 4 * D_MODEL
EPS = 1e-6
NEG = -1e30
LOG2E = float(np.log2(np.e))

A_Q_W = A_Q_HEADS * HEAD_DIM
A_KV_W = A_KV_HEADS * HEAD_DIM
B_W = B_HEADS * HEAD_DIM
IN_WIDTH = A_Q_W + 2 * A_KV_W + 3 * B_W + 2 * D_MODEL

SRC_QA = 0
SRC_KVA = SRC_QA + A_Q_W
SRC_QB = SRC_KVA + 2 * A_KV_W
SRC_KB = SRC_QB + B_W
SRC_VB = SRC_KB + B_W
SRC_GA = SRC_VB + B_W
SRC_GB = SRC_GA + D_MODEL
assert SRC_GB + D_MODEL == IN_WIDTH

COL_GA = 0
COL_GB = COL_GA + D_MODEL
COL_QA = COL_GB + D_MODEL
COL_KVA = COL_QA + A_Q_W
PROJ_W = COL_KVA + 2 * A_KV_W

V7X_VMEM_BYTES = 64 * 1024 * 1024
VMEM_LIMIT_BYTES = V7X_VMEM_BYTES - 8 * 1024 * 1024

PROJ_TM = 256
PROJ_TN = 512
WIN_TQ = 8 * BLK
WIN_LOOKAHEAD = 1
NBR_LOOKAHEAD = 6
MERGE_TM = 512
MERGE_SUB = 256
MLP_TM = 1024
MLP_TF = 1024

BF16 = jnp.bfloat16
F32 = jnp.float32


def _rms_scale(x):
    return lax.rsqrt(jnp.mean(x * x, axis=-1, keepdims=True) + EPS)


def _issue_ahead(n, depth, first_stage, second_stage):
    pending = [first_stage(i) for i in range(min(depth, n))]
    for i in range(n):
        if i + depth < n:
            pending.append(first_stage(i + depth))
        second_stage(i, pending.pop(0))


def _head_rmsnorm(a):
    parts = []
    for c in range(a.shape[1] // HEAD_DIM):
        blk = a[:, c * HEAD_DIM:(c + 1) * HEAD_DIM]
        parts.append(blk * _rms_scale(blk))
    return jnp.concatenate(parts, axis=1)


def _in_proj_kernel(x_ref, g_ref, w_ref, cg_ref, o_ref, qvb_ref, kbt_ref, h_ref):
    x = x_ref[...]
    h_ref[...] = (x * _rms_scale(x) * g_ref[...]).astype(BF16)
    for c0 in range(0, IN_WIDTH, PROJ_TN):
        cols = slice(c0, c0 + PROJ_TN)
        acc = jnp.dot(h_ref[...], w_ref[:, cols], preferred_element_type=F32)
        if c0 < SRC_KVA or SRC_QB <= c0 < SRC_VB:
            acc = _head_rmsnorm(acc) * cg_ref[:, cols]
        elif c0 == SRC_KVA:
            k = _head_rmsnorm(acc[:, :A_KV_W]) * cg_ref[:, c0:c0 + A_KV_W]
            acc = jnp.concatenate([k, acc[:, A_KV_W:]], axis=1)
        if SRC_QB <= c0 < SRC_GA:
            for j in range(PROJ_TN // HEAD_DIM):
                head = acc[:, j * HEAD_DIM:(j + 1) * HEAD_DIM]
                if c0 < SRC_KB:
                    qvb_ref[(c0 - SRC_QB) // HEAD_DIM + j] = head.astype(BF16)
                elif c0 < SRC_VB:
                    kbt_ref[(c0 - SRC_KB) // HEAD_DIM + j] = head.T.astype(BF16)
                else:
                    qvb_ref[B_HEADS + (c0 - SRC_VB) // HEAD_DIM + j] = head.astype(BF16)
        else:
            if c0 < SRC_KVA:
                d0 = COL_QA + c0 - SRC_QA
            elif c0 == SRC_KVA:
                d0 = COL_KVA
            elif c0 < SRC_GB:
                d0 = COL_GA + c0 - SRC_GA
            else:
                d0 = COL_GB + c0 - SRC_GB
            o_ref[:, d0:d0 + PROJ_TN] = acc.astype(BF16)


def _in_proj(x2, g, w, colgain):
    m = x2.shape[0]
    resident = functools.partial(pl.BlockSpec, index_map=lambda i: (0, 0), pipeline_mode=pl.Buffered(1))
    return pl.pallas_call(
        _in_proj_kernel,
        grid=(m // PROJ_TM,),
        in_specs=[
            pl.BlockSpec((PROJ_TM, D_MODEL), lambda i: (i, 0)),
            resident((1, D_MODEL)),
            resident((D_MODEL, IN_WIDTH)),
            resident((1, IN_WIDTH)),
        ],
        out_specs=[
            pl.BlockSpec((PROJ_TM, PROJ_W), lambda i: (i, 0)),
            pl.BlockSpec((2 * B_HEADS, PROJ_TM, HEAD_DIM), lambda i: (0, i, 0)),
            pl.BlockSpec((B_HEADS, HEAD_DIM, PROJ_TM), lambda i: (0, 0, i)),
        ],
        out_shape=[
            jax.ShapeDtypeStruct((m, PROJ_W), BF16),
            jax.ShapeDtypeStruct((2 * B_HEADS, m, HEAD_DIM), BF16),
            jax.ShapeDtypeStruct((B_HEADS, HEAD_DIM, m), BF16),
        ],
        scratch_shapes=[pltpu.VMEM((PROJ_TM, D_MODEL), BF16)],
        compiler_params=pltpu.CompilerParams(
            dimension_semantics=("parallel",), vmem_limit_bytes=VMEM_LIMIT_BYTES),
        name="in_proj",
    )(x2, g, w, colgain)


def _win_attn_kernel(sink_ref, q_ref, kvp_ref, kvm_ref, kvn_ref, bias_ref, o_ref, *, n_tiles):
    i = pl.program_id(1)
    kv = jnp.concatenate([kvp_ref[0], kvm_ref[0], kvn_ref[0]], axis=0)
    key = lax.broadcasted_iota(jnp.int32, (3 * BLK, 1), 0)
    first_key = jnp.where(i == 0, BLK, 0)
    end_key = jnp.where(i == n_tiles - 1, 2 * BLK, 3 * BLK)
    n_blk = WIN_TQ // BLK
    units = [(t, kvh) for t in range(n_blk) for kvh in range(A_KV_HEADS)]

    def scores(t, kvh):
        k = kv[t * BLK:(t + 3) * BLK, kvh * HEAD_DIM:(kvh + 1) * HEAD_DIM]
        q = jnp.concatenate(
            [q_ref[0, t * BLK:(t + 1) * BLK, (kvh * A_GROUP + g) * HEAD_DIM:(kvh * A_GROUP + g + 1) * HEAD_DIM]
             for g in range(A_GROUP)], axis=0)
        s = lax.dot_general(k, q, (((1,), (1,)), ((), ())), preferred_element_type=F32)
        s = s + bias_ref[kvh]
        if t == 0:
            s = jnp.where(key >= first_key, s, NEG)
        if t == n_blk - 1:
            s = jnp.where(key < end_key, s, NEG)
        return s, jnp.max(s, axis=0, keepdims=True)

    def finish(t, kvh, s_and_max):
        s, col_max = s_and_max
        heads = [kvh * A_GROUP + g for g in range(A_GROUP)]
        v = kv[t * BLK:(t + 3) * BLK, A_KV_W + kvh * HEAD_DIM:A_KV_W + (kvh + 1) * HEAD_DIM]
        sink = jnp.concatenate([jnp.full((1, BLK), sink_ref[h], F32) for h in heads], axis=1)
        m = jnp.maximum(col_max, sink)
        p = jnp.exp2(s - m)
        denom = jnp.sum(p, axis=0, keepdims=True) + jnp.exp2(sink - m)
        o = lax.dot_general(v, p.astype(BF16), (((0,), (0,)), ((), ())), preferred_element_type=F32) / denom
        for g, h in enumerate(heads):
            o_ref[0, h * HEAD_DIM:(h + 1) * HEAD_DIM, t * BLK:(t + 1) * BLK] = (
                o[:, g * BLK:(g + 1) * BLK].astype(BF16))

    _issue_ahead(len(units), WIN_LOOKAHEAD, lambda u: scores(*units[u]), lambda u, s: finish(*units[u], s))


def _win_attn(proj3, sink, bias_a):
    b, seq, _ = proj3.shape
    n_tiles = seq // WIN_TQ
    blk_per_tile = WIN_TQ // BLK
    last_blk = seq // BLK - 1
    kva_wide = COL_KVA // (2 * A_KV_W)
    return pl.pallas_call(
        functools.partial(_win_attn_kernel, n_tiles=n_tiles),
        grid=(b, n_tiles),
        in_specs=[
            pl.BlockSpec(memory_space=pltpu.SMEM),
            pl.BlockSpec((1, WIN_TQ, A_Q_W), lambda bi, i: (bi, i, COL_QA // A_Q_W)),
            pl.BlockSpec((1, BLK, 2 * A_KV_W),
                         lambda bi, i: (bi, jnp.maximum(i * blk_per_tile - 1, 0), kva_wide)),
            pl.BlockSpec((1, WIN_TQ, 2 * A_KV_W), lambda bi, i: (bi, i, kva_wide)),
            pl.BlockSpec((1, BLK, 2 * A_KV_W),
                         lambda bi, i: (bi, jnp.minimum((i + 1) * blk_per_tile, last_blk), kva_wide)),
            pl.BlockSpec((A_KV_HEADS, 3 * BLK, A_GROUP * BLK), lambda bi, i: (0, 0, 0)),
        ],
        out_specs=pl.BlockSpec((1, A_Q_W, WIN_TQ), lambda bi, i: (bi, 0, i)),
        out_shape=jax.ShapeDtypeStruct((b, A_Q_W, seq), BF16),
        compiler_params=pltpu.CompilerParams(
            dimension_semantics=("parallel", "arbitrary"), vmem_limit_bytes=VMEM_LIMIT_BYTES),
        name="win_attn",
    )(sink, proj3, proj3, proj3, proj3, bias_a)


def _nbr_attn_kernel(q_ref, kt_ref, v_ref, bias_ref, o_ref, *, rows):
    def first_key_row(r):
        return min(max(r - NA_KH // 2, 0), rows - NA_KH)

    kt_even = kt_ref[0]
    kt_odd = jnp.concatenate([kt_even[:, GRID_W:], kt_even[:, :GRID_W]], axis=1)

    def scores(r):
        rs = first_key_row(r)
        k0 = (rs - rs % 2) * GRID_W
        kt = (kt_odd if rs % 2 else kt_even)[:, k0:k0 + NA_KH * GRID_W]
        q = q_ref[0, r * GRID_W:(r + 1) * GRID_W, :]
        s = jnp.dot(q, kt, preferred_element_type=F32) + bias_ref[0, r - rs]
        return s, jnp.max(s, axis=-1, keepdims=True)

    def finish(r, s_and_max):
        s, m = s_and_max
        rs = first_key_row(r)
        v = v_ref[0, rs * GRID_W:(rs + NA_KH) * GRID_W, :]
        p = jnp.exp2(s - m)
        denom = jnp.sum(p, axis=-1, keepdims=True)
        o = jnp.dot(p.astype(BF16), v, preferred_element_type=F32) / denom
        o_ref[0, r * GRID_W:(r + 1) * GRID_W, :] = o.astype(BF16)

    _issue_ahead(rows, NBR_LOOKAHEAD, scores, finish)


def _nbr_attn(qv_b, kt_b, b, seq, bias_b):
    rows = seq // GRID_W
    return pl.pallas_call(
        functools.partial(_nbr_attn_kernel, rows=rows),
        grid=(b, B_HEADS),
        in_specs=[
            pl.BlockSpec((1, seq, HEAD_DIM), lambda bi, h: (h, bi, 0)),
            pl.BlockSpec((1, HEAD_DIM, seq), lambda bi, h: (h, 0, bi)),
            pl.BlockSpec((1, seq, HEAD_DIM), lambda bi, h: (B_HEADS + h, bi, 0)),
            pl.BlockSpec((1, NA_KH, GRID_W, NA_KH * GRID_W), lambda bi, h: (h, 0, 0, 0)),
        ],
        out_specs=pl.BlockSpec((1, seq, HEAD_DIM), lambda bi, h: (h, bi, 0)),
        out_shape=jax.ShapeDtypeStruct((B_HEADS, b * seq, HEAD_DIM), BF16),
        compiler_params=pltpu.CompilerParams(
            dimension_semantics=("parallel", "arbitrary"), vmem_limit_bytes=VMEM_LIMIT_BYTES),
        name="nbr_attn",
    )(qv_b, kt_b, qv_b, bias_b)


def _merge_kernel(oat_ref, ob_ref, ga_ref, gb_ref, x_ref, wa_ref, wb_ref, wo_ref, o_ref):
    merged = []
    for r0 in range(0, MERGE_TM, MERGE_SUB):
        rows = slice(r0, r0 + MERGE_SUB)
        ya = lax.dot_general(oat_ref[0, :, rows], wa_ref[...], (((0,), (0,)), ((), ())), preferred_element_type=F32)
        ob = jnp.concatenate([ob_ref[h, rows, :] for h in range(B_HEADS)], axis=1)
        yb = jnp.dot(ob, wb_ref[...], preferred_element_type=F32)
        gated = (jax.nn.sigmoid(ga_ref[rows, :].astype(F32)) * ya
                 + jax.nn.sigmoid(gb_ref[rows, :].astype(F32)) * yb)
        merged.append(gated.astype(BF16))
    for r0, mrg in zip(range(0, MERGE_TM, MERGE_SUB), merged):
        rows = slice(r0, r0 + MERGE_SUB)
        o_ref[rows, :] = (x_ref[rows, :] + jnp.dot(mrg, wo_ref[...], preferred_element_type=F32)).astype(BF16)


def _merge(oat, ob, proj, x2, wa, wb, wo):
    m = x2.shape[0]
    tiles_per_seq = oat.shape[2] // MERGE_TM
    resident = functools.partial(pl.BlockSpec, index_map=lambda i: (0, 0), pipeline_mode=pl.Buffered(1))
    return pl.pallas_call(
        _merge_kernel,
        grid=(m // MERGE_TM,),
        in_specs=[
            pl.BlockSpec((1, A_Q_W, MERGE_TM), lambda i: (i // tiles_per_seq, 0, i % tiles_per_seq)),
            pl.BlockSpec((B_HEADS, MERGE_TM, HEAD_DIM), lambda i: (0, i, 0)),
            pl.BlockSpec((MERGE_TM, D_MODEL), lambda i: (i, COL_GA // D_MODEL)),
            pl.BlockSpec((MERGE_TM, D_MODEL), lambda i: (i, COL_GB // D_MODEL)),
            pl.BlockSpec((MERGE_TM, D_MODEL), lambda i: (i, 0)),
            resident((A_Q_W, D_MODEL)),
            resident((B_W, D_MODEL)),
            resident((D_MODEL, D_MODEL)),
        ],
        out_specs=pl.BlockSpec((MERGE_TM, D_MODEL), lambda i: (i, 0)),
        out_shape=jax.ShapeDtypeStruct((m, D_MODEL), BF16),
        compiler_params=pltpu.CompilerParams(
            dimension_semantics=("parallel",), vmem_limit_bytes=VMEM_LIMIT_BYTES),
        name="merge",
    )(oat, ob, proj, proj, x2, wa, wb, wo)


def _mlp_kernel(x_ref, g_ref, wu_ref, wd_ref, o_ref, h_ref):
    @pl.when(pl.program_id(1) == 0)
    def _():
        x = x_ref[...].astype(F32)
        h_ref[...] = (x * _rms_scale(x) * g_ref[...]).astype(BF16)
        o_ref[...] = x

    u = jnp.dot(h_ref[...], wu_ref[...], preferred_element_type=F32)
    a = jnp.square(jnp.maximum(u, 0.0)).astype(BF16)
    o_ref[...] += jnp.dot(a, wd_ref[...], preferred_element_type=F32)


def _mlp(x2, g, wu, wd):
    m = x2.shape[0]
    return pl.pallas_call(
        _mlp_kernel,
        grid=(m // MLP_TM, D_FF // MLP_TF),
        in_specs=[
            pl.BlockSpec((MLP_TM, D_MODEL), lambda i, f: (i, 0)),
            pl.BlockSpec((1, D_MODEL), lambda i, f: (0, 0)),
            pl.BlockSpec((D_MODEL, MLP_TF), lambda i, f: (0, f)),
            pl.BlockSpec((MLP_TF, D_MODEL), lambda i, f: (f, 0)),
        ],
        out_specs=pl.BlockSpec((MLP_TM, D_MODEL), lambda i, f: (i, 0)),
        out_shape=jax.ShapeDtypeStruct((m, D_MODEL), F32),
        scratch_shapes=[pltpu.VMEM((MLP_TM, D_MODEL), BF16)],
        compiler_params=pltpu.CompilerParams(
            dimension_semantics=("parallel", "arbitrary"), vmem_limit_bytes=VMEM_LIMIT_BYTES),
        name="mlp",
    )(x2, g, wu, wd)


def _t5_bucket(rel):
    nb = N_BUCKETS // 2
    max_exact = nb // 2
    ret = (rel > 0).astype(np.int32) * nb
    n = np.abs(rel).astype(np.int32)
    nf = np.maximum(n, max_exact).astype(np.float32)
    large = max_exact + (np.log(nf / max_exact) / np.log(MAX_DISTANCE / max_exact) * (nb - max_exact)).astype(np.int32)
    large = np.minimum(large, nb - 1)
    return ret + np.where(n < max_exact, n, large)


def _select_rows(onehot, table):
    return jnp.einsum("ij,j...->i...", jnp.asarray(onehot, F32), table.astype(F32), precision=lax.Precision.HIGHEST)


def _window_bias_table(t5_bias):
    n_off = 4 * BLK
    offsets = np.arange(n_off - 1) - (2 * BLK - 1)
    onehot = _t5_bucket(offsets)[:, None] == np.arange(N_BUCKETS)[None, :]
    per_offset = jnp.where((np.abs(offsets) <= WINDOW)[:, None], _select_rows(onehot, t5_bias), NEG)
    y = jnp.pad(per_offset.T, ((0, 0), (0, 1)))
    skew = jnp.tile(y, (1, BLK + 1))[:, :BLK * (n_off + 1)].reshape(A_Q_HEADS, BLK, n_off + 1)
    table = skew[:, ::-1, :3 * BLK].reshape(A_KV_HEADS, A_GROUP, BLK, 3 * BLK)
    return jnp.transpose(table, (0, 3, 1, 2)).reshape(A_KV_HEADS, 3 * BLK, A_GROUP * BLK)


def _nbr_bias_table(rpb):
    qc = np.arange(GRID_W)[:, None]
    kc = np.arange(GRID_W)[None, :]
    start_c = np.clip(qc - NA_KW // 2, 0, GRID_W - NA_KW)
    col_mask = (kc >= start_c) & (kc < start_c + NA_KW)
    dcc = np.clip(kc - qc, -(NA_KW - 1), NA_KW - 1) + (NA_KW - 1)
    onehot = dcc.reshape(-1, 1) == np.arange(2 * NA_KW - 1)[None, :]
    by_col = _select_rows(onehot, jnp.transpose(rpb, (2, 0, 1)))
    by_col = jnp.where(col_mask.reshape(-1, 1, 1), by_col, NEG).reshape(GRID_W, GRID_W, B_HEADS, 2 * NA_KH - 1)
    by_col = jnp.transpose(by_col, (2, 0, 3, 1))
    variants = [by_col[:, :, NA_KH - 1 - d:2 * NA_KH - 1 - d, :] for d in range(NA_KH)]
    return jnp.stack(variants, axis=1).reshape(B_HEADS, NA_KH, GRID_W, NA_KH * GRID_W)


def _prepare(norm_mix_g, w_in, q_norm_a, k_norm_a, t5_bias, sink_a, q_norm_b, k_norm_b, rpb_b,
             w_br_a, w_br_b, w_out, norm_mlp_g, w_up, w_down):
    scale = HEAD_DIM ** -0.5
    colgain = jnp.concatenate([
        jnp.tile(q_norm_a[0].astype(F32), A_Q_HEADS) * (scale * LOG2E),
        jnp.tile(k_norm_a[0].astype(F32), A_KV_HEADS),
        jnp.ones((A_KV_W,), F32),
        jnp.tile(q_norm_b[0].astype(F32), B_HEADS) * (scale * LOG2E),
        jnp.tile(k_norm_b[0].astype(F32), B_HEADS),
        jnp.ones((B_W + 2 * D_MODEL,), F32)]).reshape(1, IN_WIDTH)
    return dict(
        g_mix=norm_mix_g[0].astype(F32).reshape(1, D_MODEL),
        w_in=w_in[0].astype(BF16),
        colgain=colgain,
        sink=sink_a[0].astype(F32) * LOG2E,
        bias_a=_window_bias_table(t5_bias * LOG2E),
        bias_b=_nbr_bias_table(rpb_b[0] * LOG2E),
        w_a=w_br_a[0].astype(BF16),
        w_b=w_br_b[0].astype(BF16),
        w_o=w_out[0].astype(BF16),
        g_mlp=norm_mlp_g[0].astype(F32).reshape(1, D_MODEL),
        w_up=w_up[0].astype(BF16),
        w_down=w_down[0].astype(BF16),
    )


def _layer(x, p):
    b, seq, _ = x.shape
    x2 = x.reshape(b * seq, D_MODEL)
    proj, qv_b, kt_b = _in_proj(x2, p["g_mix"], p["w_in"], p["colgain"])
    proj3 = proj.reshape(b, seq, PROJ_W)
    oat = _win_attn(proj3, p["sink"], p["bias_a"])
    ob = _nbr_attn(qv_b, kt_b, b, seq, p["bias_b"])
    x1 = _merge(oat, ob, proj, x2, p["w_a"], p["w_b"], p["w_o"])
    y = _mlp(x1, p["g_mlp"], p["w_up"], p["w_down"])
    return y.reshape(b, seq, D_MODEL)


def kernel(x_prompt, x_sample, norm_mix_g, w_in, q_norm_a, k_norm_a, t5_bias, sink_a, q_norm_b, k_norm_b, rpb_b,
           w_br_a, w_br_b, w_out, norm_mlp_g, w_up, w_down):
    p = _prepare(norm_mix_g, w_in, q_norm_a, k_norm_a, t5_bias, sink_a, q_norm_b, k_norm_b, rpb_b,
                 w_br_a, w_br_b, w_out, norm_mlp_g, w_up, w_down)
    return (_layer(x_prompt, p), _layer(x_sample, p))
```

```python
import functools

import jax
import jax.numpy as jnp
import numpy as np
from jax import lax
from jax.experimental import pallas as pl
from jax.experimental.pallas import tpu as pltpu

D_MODEL = 2048
HEAD_DIM = 128
A_Q_HEADS = 8
A_KV_HEADS = 2
A_GROUP = A_Q_HEADS // A_KV_HEADS
WINDOW = 128
BLK = WINDOW
N_BUCKETS = 32
MAX_DISTANCE = 128
B_HEADS = 8
GRID_W = 64
NA_KH = 8
NA_KW = 16
D_FF = 4 * D_MODEL
EPS = 1e-6
NEG = -1e30
LOG2E = float(np.log2(np.e))

A_Q_W = A_Q_HEADS * HEAD_DIM
A_KV_W = A_KV_HEADS * HEAD_DIM
B_W = B_HEADS * HEAD_DIM
IN_WIDTH = A_Q_W + 2 * A_KV_W + 3 * B_W + 2 * D_MODEL

SRC_QA = 0
SRC_KVA = SRC_QA + A_Q_W
SRC_QB = SRC_KVA + 2 * A_KV_W
SRC_KB = SRC_QB + B_W
SRC_VB = SRC_KB + B_W
SRC_GA = SRC_VB + B_W
SRC_GB = SRC_GA + D_MODEL
assert SRC_GB + D_MODEL == IN_WIDTH

COL_GA = 0
COL_GB = COL_GA + D_MODEL
COL_QA = COL_GB + D_MODEL
COL_KVA = COL_QA + A_Q_W
PROJ_W = COL_KVA + 2 * A_KV_W

V7X_VMEM_BYTES = 64 * 1024 * 1024
VMEM_COMPILER_RESERVE_BYTES = 8 * 1024 * 1024
VMEM_LIMIT_BYTES = V7X_VMEM_BYTES - VMEM_COMPILER_RESERVE_BYTES

PROJ_TM = 256
PROJ_TN = 512
WIN_TQ = 16 * BLK
WIN_LOOKAHEAD = 1
NBR_LOOKAHEAD = 6
NBR_HEADS = 2
MERGE_TM = 512
MERGE_SUB = 256
MLP_TM = 1024
MLP_TF = 1024

BF16 = jnp.bfloat16
F32 = jnp.float32


def _rms_scale(x):
    return lax.rsqrt(jnp.mean(x * x, axis=-1, keepdims=True) + EPS)


def _issue_ahead(n, depth, first_stage, second_stage):
    pending = [first_stage(i) for i in range(min(depth, n))]
    for i in range(n):
        if i + depth < n:
            pending.append(first_stage(i + depth))
        second_stage(i, pending.pop(0))


def _head_rmsnorm(a):
    parts = []
    for c in range(a.shape[1] // HEAD_DIM):
        blk = a[:, c * HEAD_DIM:(c + 1) * HEAD_DIM]
        parts.append(blk * _rms_scale(blk))
    return jnp.concatenate(parts, axis=1)


def _in_proj_kernel(x_ref, g_ref, w_ref, cg_ref, o_ref, qvb_ref, kbt_ref):
    x = x_ref[...]
    h = (x * _rms_scale(x) * g_ref[...]).astype(BF16)
    for c0 in range(0, IN_WIDTH, PROJ_TN):
        cols = slice(c0, c0 + PROJ_TN)
        acc = jnp.dot(h, w_ref[:, cols], preferred_element_type=F32)
        if c0 < SRC_KVA or SRC_QB <= c0 < SRC_VB:
            acc = _head_rmsnorm(acc) * cg_ref[:, cols]
        elif c0 == SRC_KVA:
            k = _head_rmsnorm(acc[:, :A_KV_W]) * cg_ref[:, c0:c0 + A_KV_W]
            acc = jnp.concatenate([k, acc[:, A_KV_W:]], axis=1)
        if SRC_QB <= c0 < SRC_GA:
            for j in range(PROJ_TN // HEAD_DIM):
                head = acc[:, j * HEAD_DIM:(j + 1) * HEAD_DIM]
                if c0 < SRC_KB:
                    qvb_ref[(c0 - SRC_QB) // HEAD_DIM + j] = head.astype(BF16)
                elif c0 < SRC_VB:
                    kbt_ref[(c0 - SRC_KB) // HEAD_DIM + j] = head.T.astype(BF16)
                else:
                    qvb_ref[B_HEADS + (c0 - SRC_VB) // HEAD_DIM + j] = head.astype(BF16)
        else:
            if c0 < SRC_KVA:
                d0 = COL_QA + c0 - SRC_QA
            elif c0 == SRC_KVA:
                d0 = COL_KVA
            elif c0 < SRC_GB:
                d0 = COL_GA + c0 - SRC_GA
            else:
                d0 = COL_GB + c0 - SRC_GB
            o_ref[:, d0:d0 + PROJ_TN] = acc.astype(BF16)


def _in_proj(x2, g, w, colgain):
    m = x2.shape[0]
    resident = functools.partial(pl.BlockSpec, index_map=lambda i: (0, 0), pipeline_mode=pl.Buffered(1))
    return pl.pallas_call(
        _in_proj_kernel,
        grid=(m // PROJ_TM,),
        in_specs=[
            pl.BlockSpec((PROJ_TM, D_MODEL), lambda i: (i, 0)),
            resident((1, D_MODEL)),
            resident((D_MODEL, IN_WIDTH)),
            resident((1, IN_WIDTH)),
        ],
        out_specs=[
            pl.BlockSpec((PROJ_TM, PROJ_W), lambda i: (i, 0)),
            pl.BlockSpec((2 * B_HEADS, PROJ_TM, HEAD_DIM), lambda i: (0, i, 0)),
            pl.BlockSpec((B_HEADS, HEAD_DIM, PROJ_TM), lambda i: (0, 0, i)),
        ],
        out_shape=[
            jax.ShapeDtypeStruct((m, PROJ_W), BF16),
            jax.ShapeDtypeStruct((2 * B_HEADS, m, HEAD_DIM), BF16),
            jax.ShapeDtypeStruct((B_HEADS, HEAD_DIM, m), BF16),
        ],
        compiler_params=pltpu.CompilerParams(
            dimension_semantics=("parallel",), vmem_limit_bytes=VMEM_LIMIT_BYTES),
        name="in_proj",
    )(x2, g, w, colgain)


def _win_attn_kernel(sink_ref, q_ref, kvp_ref, kvm_ref, kvn_ref, bias_ref, o_ref, *, n_tiles):
    i = pl.program_id(1)
    kv = jnp.concatenate([kvp_ref[0], kvm_ref[0], kvn_ref[0]], axis=0)
    key = lax.broadcasted_iota(jnp.int32, (3 * BLK, 1), 0)
    first_key = jnp.where(i == 0, BLK, 0)
    end_key = jnp.where(i == n_tiles - 1, 2 * BLK, 3 * BLK)
    n_blk = WIN_TQ // BLK
    units = [(t, kvh) for t in range(n_blk) for kvh in range(A_KV_HEADS)]

    def scores(t, kvh):
        k = kv[t * BLK:(t + 3) * BLK, kvh * HEAD_DIM:(kvh + 1) * HEAD_DIM]
        q = jnp.concatenate(
            [q_ref[0, t * BLK:(t + 1) * BLK, (kvh * A_GROUP + g) * HEAD_DIM:(kvh * A_GROUP + g + 1) * HEAD_DIM]
             for g in range(A_GROUP)], axis=0)
        s = lax.dot_general(k, q, (((1,), (1,)), ((), ())), preferred_element_type=F32)
        s = s + bias_ref[kvh]
        if t == 0:
            s = jnp.where(key >= first_key, s, NEG)
        if t == n_blk - 1:
            s = jnp.where(key < end_key, s, NEG)
        return s, jnp.max(s, axis=0, keepdims=True)

    def finish(t, kvh, s_and_max):
        s, col_max = s_and_max
        heads = [kvh * A_GROUP + g for g in range(A_GROUP)]
        v = kv[t * BLK:(t + 3) * BLK, A_KV_W + kvh * HEAD_DIM:A_KV_W + (kvh + 1) * HEAD_DIM]
        sink = jnp.concatenate([jnp.full((1, BLK), sink_ref[h], F32) for h in heads], axis=1)
        m = jnp.maximum(col_max, sink)
        p = jnp.exp2(s - m)
        denom = jnp.sum(p, axis=0, keepdims=True) + jnp.exp2(sink - m)
        o = lax.dot_general(v, p.astype(BF16), (((0,), (0,)), ((), ())), preferred_element_type=F32) / denom
        for g, h in enumerate(heads):
            o_ref[0, h * HEAD_DIM:(h + 1) * HEAD_DIM, t * BLK:(t + 1) * BLK] = (
                o[:, g * BLK:(g + 1) * BLK].astype(BF16))

    _issue_ahead(len(units), WIN_LOOKAHEAD, lambda u: scores(*units[u]), lambda u, s: finish(*units[u], s))


def _win_attn(proj3, sink, bias_a):
    b, seq, _ = proj3.shape
    n_tiles = seq // WIN_TQ
    blk_per_tile = WIN_TQ // BLK
    last_blk = seq // BLK - 1
    kva_wide = COL_KVA // (2 * A_KV_W)
    return pl.pallas_call(
        functools.partial(_win_attn_kernel, n_tiles=n_tiles),
        grid=(b, n_tiles),
        in_specs=[
            pl.BlockSpec(memory_space=pltpu.SMEM),
            pl.BlockSpec((1, WIN_TQ, A_Q_W), lambda bi, i: (bi, i, COL_QA // A_Q_W)),
            pl.BlockSpec((1, BLK, 2 * A_KV_W),
                         lambda bi, i: (bi, jnp.maximum(i * blk_per_tile - 1, 0), kva_wide)),
            pl.BlockSpec((1, WIN_TQ, 2 * A_KV_W), lambda bi, i: (bi, i, kva_wide)),
            pl.BlockSpec((1, BLK, 2 * A_KV_W),
                         lambda bi, i: (bi, jnp.minimum((i + 1) * blk_per_tile, last_blk), kva_wide)),
            pl.BlockSpec((A_KV_HEADS, 3 * BLK, A_GROUP * BLK), lambda bi, i: (0, 0, 0)),
        ],
        out_specs=pl.BlockSpec((1, A_Q_W, WIN_TQ), lambda bi, i: (bi, 0, i)),
        out_shape=jax.ShapeDtypeStruct((b, A_Q_W, seq), BF16),
        compiler_params=pltpu.CompilerParams(
            dimension_semantics=("parallel", "arbitrary"), vmem_limit_bytes=VMEM_LIMIT_BYTES),
        name="win_attn",
    )(sink, proj3, proj3, proj3, proj3, bias_a)


def _nbr_attn_kernel(q_ref, kt_ref, v_ref, bias_ref, o_ref, *, rows):
    def first_key_row(r):
        return min(max(r - NA_KH // 2, 0), rows - NA_KH)

    kt_even = [kt_ref[h] for h in range(NBR_HEADS)]
    kt_odd = [jnp.concatenate([kt[:, GRID_W:], kt[:, :GRID_W]], axis=1) for kt in kt_even]
    units = [(h, r) for h in range(NBR_HEADS) for r in range(rows)]

    def scores(h, r):
        rs = first_key_row(r)
        k0 = (rs - rs % 2) * GRID_W
        kt = (kt_odd if rs % 2 else kt_even)[h][:, k0:k0 + NA_KH * GRID_W]
        q = q_ref[h, r * GRID_W:(r + 1) * GRID_W, :]
        s = jnp.dot(q, kt, preferred_element_type=F32) + bias_ref[h, r - rs]
        return s, jnp.max(s, axis=-1, keepdims=True)

    def finish(h, r, s_and_max):
        s, m = s_and_max
        rs = first_key_row(r)
        v = v_ref[h, rs * GRID_W:(rs + NA_KH) * GRID_W, :]
        p = jnp.exp2(s - m)
        denom = jnp.sum(p, axis=-1, keepdims=True)
        o = jnp.dot(p.astype(BF16), v, preferred_element_type=F32) / denom
        o_ref[h, r * GRID_W:(r + 1) * GRID_W, :] = o.astype(BF16)

    _issue_ahead(len(units), NBR_LOOKAHEAD, lambda u: scores(*units[u]), lambda u, s: finish(*units[u], s))


def _nbr_attn(qv_b, kt_b, b, seq, bias_b):
    rows = seq // GRID_W
    return pl.pallas_call(
        functools.partial(_nbr_attn_kernel, rows=rows),
        grid=(b, B_HEADS // NBR_HEADS),
        in_specs=[
            pl.BlockSpec((NBR_HEADS, seq, HEAD_DIM), lambda bi, h: (h, bi, 0)),
            pl.BlockSpec((NBR_HEADS, HEAD_DIM, seq), lambda bi, h: (h, 0, bi)),
            pl.BlockSpec((NBR_HEADS, seq, HEAD_DIM), lambda bi, h: (B_HEADS // NBR_HEADS + h, bi, 0)),
            pl.BlockSpec((NBR_HEADS, NA_KH, GRID_W, NA_KH * GRID_W), lambda bi, h: (h, 0, 0, 0)),
        ],
        out_specs=pl.BlockSpec((NBR_HEADS, seq, HEAD_DIM), lambda bi, h: (h, bi, 0)),
        out_shape=jax.ShapeDtypeStruct((B_HEADS, b * seq, HEAD_DIM), BF16),
        compiler_params=pltpu.CompilerParams(
            dimension_semantics=("parallel", "arbitrary"), vmem_limit_bytes=VMEM_LIMIT_BYTES),
        name="nbr_attn",
    )(qv_b, kt_b, qv_b, bias_b)


def _merge_kernel(oat_ref, ob_ref, ga_ref, gb_ref, x_ref, wa_ref, wb_ref, wo_ref, o_ref):
    merged = []
    for r0 in range(0, MERGE_TM, MERGE_SUB):
        rows = slice(r0, r0 + MERGE_SUB)
        ya = lax.dot_general(oat_ref[0, :, rows], wa_ref[...], (((0,), (0,)), ((), ())), preferred_element_type=F32)
        ob = jnp.concatenate([ob_ref[h, rows, :] for h in range(B_HEADS)], axis=1)
        yb = jnp.dot(ob, wb_ref[...], preferred_element_type=F32)
        gated = (jax.nn.sigmoid(ga_ref[rows, :].astype(F32)) * ya
                 + jax.nn.sigmoid(gb_ref[rows, :].astype(F32)) * yb)
        merged.append(gated.astype(BF16))
    for r0, mrg in zip(range(0, MERGE_TM, MERGE_SUB), merged):
        rows = slice(r0, r0 + MERGE_SUB)
        o_ref[rows, :] = (x_ref[rows, :] + jnp.dot(mrg, wo_ref[...], preferred_element_type=F32)).astype(BF16)


def _merge(oat, ob, proj, x2, wa, wb, wo):
    m = x2.shape[0]
    tiles_per_seq = oat.shape[2] // MERGE_TM
    resident = functools.partial(pl.BlockSpec, index_map=lambda i: (0, 0), pipeline_mode=pl.Buffered(1))
    return pl.pallas_call(
        _merge_kernel,
        grid=(m // MERGE_TM,),
        in_specs=[
            pl.BlockSpec((1, A_Q_W, MERGE_TM), lambda i: (i // tiles_per_seq, 0, i % tiles_per_seq)),
            pl.BlockSpec((B_HEADS, MERGE_TM, HEAD_DIM), lambda i: (0, i, 0)),
            pl.BlockSpec((MERGE_TM, D_MODEL), lambda i: (i, COL_GA // D_MODEL)),
            pl.BlockSpec((MERGE_TM, D_MODEL), lambda i: (i, COL_GB // D_MODEL)),
            pl.BlockSpec((MERGE_TM, D_MODEL), lambda i: (i, 0)),
            resident((A_Q_W, D_MODEL)),
            resident((B_W, D_MODEL)),
            resident((D_MODEL, D_MODEL)),
        ],
        out_specs=pl.BlockSpec((MERGE_TM, D_MODEL), lambda i: (i, 0)),
        out_shape=jax.ShapeDtypeStruct((m, D_MODEL), BF16),
        compiler_params=pltpu.CompilerParams(
            dimension_semantics=("parallel",), vmem_limit_bytes=VMEM_LIMIT_BYTES),
        name="merge",
    )(oat, ob, proj, proj, x2, wa, wb, wo)


def _mlp_kernel(x_ref, g_ref, wu_ref, wd_ref, o_ref, h_ref):
    @pl.when(pl.program_id(1) == 0)
    def _():
        x = x_ref[...].astype(F32)
        h_ref[...] = (x * _rms_scale(x) * g_ref[...]).astype(BF16)
        o_ref[...] = x

    u = jnp.dot(h_ref[...], wu_ref[...], preferred_element_type=F32)
    a = jnp.square(jnp.maximum(u, 0.0)).astype(BF16)
    o_ref[...] += jnp.dot(a, wd_ref[...], preferred_element_type=F32)


def _mlp(x2, g, wu, wd):
    m = x2.shape[0]
    return pl.pallas_call(
        _mlp_kernel,
        grid=(m // MLP_TM, D_FF // MLP_TF),
        in_specs=[
            pl.BlockSpec((MLP_TM, D_MODEL), lambda i, f: (i, 0)),
            pl.BlockSpec((1, D_MODEL), lambda i, f: (0, 0)),
            pl.BlockSpec((D_MODEL, MLP_TF), lambda i, f: (0, f)),
            pl.BlockSpec((MLP_TF, D_MODEL), lambda i, f: (f, 0)),
        ],
        out_specs=pl.BlockSpec((MLP_TM, D_MODEL), lambda i, f: (i, 0)),
        out_shape=jax.ShapeDtypeStruct((m, D_MODEL), F32),
        scratch_shapes=[pltpu.VMEM((MLP_TM, D_MODEL), BF16)],
        compiler_params=pltpu.CompilerParams(
            dimension_semantics=("parallel", "arbitrary"), vmem_limit_bytes=VMEM_LIMIT_BYTES),
        name="mlp",
    )(x2, g, wu, wd)


def _t5_bucket(rel):
    nb = N_BUCKETS // 2
    max_exact = nb // 2
    ret = (rel > 0).astype(np.int32) * nb
    n = np.abs(rel).astype(np.int32)
    nf = np.maximum(n, max_exact).astype(np.float32)
    large = max_exact + (np.log(nf / max_exact) / np.log(MAX_DISTANCE / max_exact) * (nb - max_exact)).astype(np.int32)
    large = np.minimum(large, nb - 1)
    return ret + np.where(n < max_exact, n, large)


def _select_rows(onehot, table):
    return jnp.einsum("ij,j...->i...", jnp.asarray(onehot, F32), table.astype(F32), precision=lax.Precision.HIGHEST)


def _window_bias_table(t5_bias):
    n_off = 4 * BLK
    offsets = np.arange(n_off - 1) - (2 * BLK - 1)
    onehot = _t5_bucket(offsets)[:, None] == np.arange(N_BUCKETS)[None, :]
    per_offset = jnp.where((np.abs(offsets) <= WINDOW)[:, None], _select_rows(onehot, t5_bias), NEG)
    y = jnp.pad(per_offset.T, ((0, 0), (0, 1)))
    skew = jnp.tile(y, (1, BLK + 1))[:, :BLK * (n_off + 1)].reshape(A_Q_HEADS, BLK, n_off + 1)
    table = skew[:, ::-1, :3 * BLK].reshape(A_KV_HEADS, A_GROUP, BLK, 3 * BLK)
    return jnp.transpose(table, (0, 3, 1, 2)).reshape(A_KV_HEADS, 3 * BLK, A_GROUP * BLK)


def _nbr_bias_table(rpb):
    qc = np.arange(GRID_W)[:, None]
    kc = np.arange(GRID_W)[None, :]
    start_c = np.clip(qc - NA_KW // 2, 0, GRID_W - NA_KW)
    col_mask = (kc >= start_c) & (kc < start_c + NA_KW)
    dcc = np.clip(kc - qc, -(NA_KW - 1), NA_KW - 1) + (NA_KW - 1)
    d = np.arange(NA_KH)[:, None]
    j = np.arange(NA_KH)[None, :]
    pick_row = (j - d + NA_KH - 1)[:, :, None] == np.arange(2 * NA_KH - 1)
    pick_col = dcc[:, :, None] == np.arange(2 * NA_KW - 1)
    exact = dict(precision=lax.Precision.HIGHEST)
    by_row = jnp.einsum("djr,hrw->hdjw", jnp.asarray(pick_row, F32), rpb.astype(F32), **exact)
    table = jnp.einsum("hdjw,ckw->hdcjk", by_row, jnp.asarray(pick_col, F32), **exact)
    table = jnp.where(col_mask[None, None, :, None, :], table, NEG)
    return table.reshape(B_HEADS, NA_KH, GRID_W, NA_KH * GRID_W)


def _prepare(norm_mix_g, w_in, q_norm_a, k_norm_a, t5_bias, sink_a, q_norm_b, k_norm_b, rpb_b,
             w_br_a, w_br_b, w_out, norm_mlp_g, w_up, w_down):
    scale = HEAD_DIM ** -0.5
    colgain = jnp.concatenate([
        jnp.tile(q_norm_a[0].astype(F32), A_Q_HEADS) * (scale * LOG2E),
        jnp.tile(k_norm_a[0].astype(F32), A_KV_HEADS),
        jnp.ones((A_KV_W,), F32),
        jnp.tile(q_norm_b[0].astype(F32), B_HEADS) * (scale * LOG2E),
        jnp.tile(k_norm_b[0].astype(F32), B_HEADS),
        jnp.ones((B_W + 2 * D_MODEL,), F32)]).reshape(1, IN_WIDTH)
    return dict(
        g_mix=norm_mix_g[0].astype(F32).reshape(1, D_MODEL),
        w_in=w_in[0].astype(BF16),
        colgain=colgain,
        sink=sink_a[0].astype(F32) * LOG2E,
        bias_a=_window_bias_table(t5_bias * LOG2E),
        bias_b=_nbr_bias_table(rpb_b[0] * LOG2E),
        w_a=w_br_a[0].astype(BF16),
        w_b=w_br_b[0].astype(BF16),
        w_o=w_out[0].astype(BF16),
        g_mlp=norm_mlp_g[0].astype(F32).reshape(1, D_MODEL),
        w_up=w_up[0].astype(BF16),
        w_down=w_down[0].astype(BF16),
    )


def _layer(x, p):
    b, seq, _ = x.shape
    x2 = x.reshape(b * seq, D_MODEL)
    proj, qv_b, kt_b = _in_proj(x2, p["g_mix"], p["w_in"], p["colgain"])
    proj3 = proj.reshape(b, seq, PROJ_W)
    oat = _win_attn(proj3, p["sink"], p["bias_a"])
    ob = _nbr_attn(qv_b, kt_b, b, seq, p["bias_b"])
    x1 = _merge(oat, ob, proj, x2, p["w_a"], p["w_b"], p["w_o"])
    y = _mlp(x1, p["g_mlp"], p["w_up"], p["w_down"])
    return y.reshape(b, seq, D_MODEL)


def kernel(x_prompt, x_sample, norm_mix_g, w_in, q_norm_a, k_norm_a, t5_bias, sink_a, q_norm_b, k_norm_b, rpb_b,
           w_br_a, w_br_b, w_out, norm_mlp_g, w_up, w_down):
    p = _prepare(norm_mix_g, w_in, q_norm_a, k_norm_a, t5_bias, sink_a, q_norm_b, k_norm_b, rpb_b,
                 w_br_a, w_br_b, w_out, norm_mlp_g, w_up, w_down)
    return (_layer(x_prompt, p), _layer(x_sample, p))
```

```python
import functools

import jax
import jax.numpy as jnp
import numpy as np
from jax import lax
from jax.experimental import pallas as pl
from jax.experimental.pallas import tpu as pltpu

D_MODEL = 2048
HEAD_DIM = 128
A_Q_HEADS = 8
A_KV_HEADS = 2
A_GROUP = A_Q_HEADS // A_KV_HEADS
WINDOW = 128
BLK = WINDOW
N_BUCKETS = 32
MAX_DISTANCE = 128
B_HEADS = 8
GRID_W = 64
NA_KH = 8
NA_KW = 16
D_FF = 4 * D_MODEL
EPS = 1e-6
NEG = -1e30
LOG2E = float(np.log2(np.e))

A_Q_W = A_Q_HEADS * HEAD_DIM
A_KV_W = A_KV_HEADS * HEAD_DIM
B_W = B_HEADS * HEAD_DIM
IN_WIDTH = A_Q_W + 2 * A_KV_W + 3 * B_W + 2 * D_MODEL

SRC_QA = 0
SRC_KVA = SRC_QA + A_Q_W
SRC_QB = SRC_KVA + 2 * A_KV_W
SRC_KB = SRC_QB + B_W
SRC_VB = SRC_KB + B_W
SRC_GA = SRC_VB + B_W
SRC_GB = SRC_GA + D_MODEL
assert SRC_GB + D_MODEL == IN_WIDTH

COL_GA = 0
COL_GB = COL_GA + D_MODEL
COL_QA = COL_GB + D_MODEL
COL_KVA = COL_QA + A_Q_W
PROJ_W = COL_KVA + 2 * A_KV_W

V7X_VMEM_BYTES = 64 * 1024 * 1024
VMEM_COMPILER_RESERVE_BYTES = 8 * 1024 * 1024
VMEM_LIMIT_BYTES = V7X_VMEM_BYTES - VMEM_COMPILER_RESERVE_BYTES

PROJ_TM = 256
PROJ_TN = 512
WIN_TQ = 16 * BLK
WIN_LOOKAHEAD = 1
NBR_LOOKAHEAD = 6
NBR_ROWS_PER_STEP = 64
MERGE_TM = 512
MERGE_SUB = 256
MLP_TM = 1024
MLP_TF = 1024

BF16 = jnp.bfloat16
F32 = jnp.float32


def _rms_scale(x):
    return lax.rsqrt(jnp.mean(x * x, axis=-1, keepdims=True) + EPS)


def _issue_ahead(n, depth, first_stage, second_stage):
    pending = [first_stage(i) for i in range(min(depth, n))]
    for i in range(n):
        if i + depth < n:
            pending.append(first_stage(i + depth))
        second_stage(i, pending.pop(0))


def _head_rmsnorm(a):
    parts = []
    for c in range(a.shape[1] // HEAD_DIM):
        blk = a[:, c * HEAD_DIM:(c + 1) * HEAD_DIM]
        parts.append(blk * _rms_scale(blk))
    return jnp.concatenate(parts, axis=1)


def _in_proj_kernel(x_ref, g_ref, w_ref, cg_ref, o_ref, qvb_ref, kbt_ref):
    x = x_ref[...]
    h = (x * _rms_scale(x) * g_ref[...]).astype(BF16)
    for c0 in range(0, IN_WIDTH, PROJ_TN):
        cols = slice(c0, c0 + PROJ_TN)
        acc = jnp.dot(h, w_ref[:, cols], preferred_element_type=F32)
        if c0 < SRC_KVA or SRC_QB <= c0 < SRC_VB:
            acc = _head_rmsnorm(acc) * cg_ref[:, cols]
        elif c0 == SRC_KVA:
            k = _head_rmsnorm(acc[:, :A_KV_W]) * cg_ref[:, c0:c0 + A_KV_W]
            acc = jnp.concatenate([k, acc[:, A_KV_W:]], axis=1)
        if SRC_QB <= c0 < SRC_GA:
            for j in range(PROJ_TN // HEAD_DIM):
                head = acc[:, j * HEAD_DIM:(j + 1) * HEAD_DIM]
                if c0 < SRC_KB:
                    qvb_ref[(c0 - SRC_QB) // HEAD_DIM + j] = head.astype(BF16)
                elif c0 < SRC_VB:
                    kbt_ref[(c0 - SRC_KB) // HEAD_DIM + j] = head.T.astype(BF16)
                else:
                    qvb_ref[B_HEADS + (c0 - SRC_VB) // HEAD_DIM + j] = head.astype(BF16)
        else:
            if c0 < SRC_KVA:
                d0 = COL_QA + c0 - SRC_QA
            elif c0 == SRC_KVA:
                d0 = COL_KVA
            elif c0 < SRC_GB:
                d0 = COL_GA + c0 - SRC_GA
            else:
                d0 = COL_GB + c0 - SRC_GB
            o_ref[:, d0:d0 + PROJ_TN] = acc.astype(BF16)


def _in_proj(x2, g, w, colgain):
    m = x2.shape[0]
    resident = functools.partial(pl.BlockSpec, index_map=lambda i: (0, 0), pipeline_mode=pl.Buffered(1))
    return pl.pallas_call(
        _in_proj_kernel,
        grid=(m // PROJ_TM,),
        in_specs=[
            pl.BlockSpec((PROJ_TM, D_MODEL), lambda i: (i, 0)),
            resident((1, D_MODEL)),
            resident((D_MODEL, IN_WIDTH)),
            resident((1, IN_WIDTH)),
        ],
        out_specs=[
            pl.BlockSpec((PROJ_TM, PROJ_W), lambda i: (i, 0)),
            pl.BlockSpec((2 * B_HEADS, PROJ_TM, HEAD_DIM), lambda i: (0, i, 0)),
            pl.BlockSpec((B_HEADS, HEAD_DIM, PROJ_TM), lambda i: (0, 0, i)),
        ],
        out_shape=[
            jax.ShapeDtypeStruct((m, PROJ_W), BF16),
            jax.ShapeDtypeStruct((2 * B_HEADS, m, HEAD_DIM), BF16),
            jax.ShapeDtypeStruct((B_HEADS, HEAD_DIM, m), BF16),
        ],
        compiler_params=pltpu.CompilerParams(
            dimension_semantics=("parallel",), vmem_limit_bytes=VMEM_LIMIT_BYTES),
        name="in_proj",
    )(x2, g, w, colgain)


def _win_attn_kernel(sink_ref, q_ref, kvp_ref, kvm_ref, kvn_ref, bias_ref, o_ref, *, n_tiles):
    i = pl.program_id(1)
    kv = jnp.concatenate([kvp_ref[0], kvm_ref[0], kvn_ref[0]], axis=0)
    key = lax.broadcasted_iota(jnp.int32, (3 * BLK, 1), 0)
    first_key = jnp.where(i == 0, BLK, 0)
    end_key = jnp.where(i == n_tiles - 1, 2 * BLK, 3 * BLK)
    n_blk = WIN_TQ // BLK
    units = [(t, kvh) for t in range(n_blk) for kvh in range(A_KV_HEADS)]

    def scores(t, kvh):
        k = kv[t * BLK:(t + 3) * BLK, kvh * HEAD_DIM:(kvh + 1) * HEAD_DIM]
        q = jnp.concatenate(
            [q_ref[0, t * BLK:(t + 1) * BLK, (kvh * A_GROUP + g) * HEAD_DIM:(kvh * A_GROUP + g + 1) * HEAD_DIM]
             for g in range(A_GROUP)], axis=0)
        s = lax.dot_general(k, q, (((1,), (1,)), ((), ())), preferred_element_type=F32)
        s = s + bias_ref[kvh]
        if t == 0:
            s = jnp.where(key >= first_key, s, NEG)
        if t == n_blk - 1:
            s = jnp.where(key < end_key, s, NEG)
        return s, jnp.max(s, axis=0, keepdims=True)

    def finish(t, kvh, s_and_max):
        s, col_max = s_and_max
        heads = [kvh * A_GROUP + g for g in range(A_GROUP)]
        v = kv[t * BLK:(t + 3) * BLK, A_KV_W + kvh * HEAD_DIM:A_KV_W + (kvh + 1) * HEAD_DIM]
        sink = jnp.concatenate([jnp.full((1, BLK), sink_ref[h], F32) for h in heads], axis=1)
        m = jnp.maximum(col_max, sink)
        p = jnp.exp2(s - m)
        denom = jnp.sum(p, axis=0, keepdims=True) + jnp.exp2(sink - m)
        o = lax.dot_general(v, p.astype(BF16), (((0,), (0,)), ((), ())), preferred_element_type=F32) / denom
        for g, h in enumerate(heads):
            o_ref[0, h * HEAD_DIM:(h + 1) * HEAD_DIM, t * BLK:(t + 1) * BLK] = (
                o[:, g * BLK:(g + 1) * BLK].astype(BF16))

    _issue_ahead(len(units), WIN_LOOKAHEAD, lambda u: scores(*units[u]), lambda u, s: finish(*units[u], s))


def _win_attn(proj3, sink, bias_a):
    b, seq, _ = proj3.shape
    n_tiles = seq // WIN_TQ
    blk_per_tile = WIN_TQ // BLK
    last_blk = seq // BLK - 1
    kva_wide = COL_KVA // (2 * A_KV_W)
    return pl.pallas_call(
        functools.partial(_win_attn_kernel, n_tiles=n_tiles),
        grid=(b, n_tiles),
        in_specs=[
            pl.BlockSpec(memory_space=pltpu.SMEM),
            pl.BlockSpec((1, WIN_TQ, A_Q_W), lambda bi, i: (bi, i, COL_QA // A_Q_W)),
            pl.BlockSpec((1, BLK, 2 * A_KV_W),
                         lambda bi, i: (bi, jnp.maximum(i * blk_per_tile - 1, 0), kva_wide)),
            pl.BlockSpec((1, WIN_TQ, 2 * A_KV_W), lambda bi, i: (bi, i, kva_wide)),
            pl.BlockSpec((1, BLK, 2 * A_KV_W),
                         lambda bi, i: (bi, jnp.minimum((i + 1) * blk_per_tile, last_blk), kva_wide)),
            pl.BlockSpec((A_KV_HEADS, 3 * BLK, A_GROUP * BLK), lambda bi, i: (0, 0, 0)),
        ],
        out_specs=pl.BlockSpec((1, A_Q_W, WIN_TQ), lambda bi, i: (bi, 0, i)),
        out_shape=jax.ShapeDtypeStruct((b, A_Q_W, seq), BF16),
        compiler_params=pltpu.CompilerParams(
            dimension_semantics=("parallel", "arbitrary"), vmem_limit_bytes=VMEM_LIMIT_BYTES),
        name="win_attn",
    )(sink, proj3, proj3, proj3, proj3, bias_a)


def _nbr_attn_kernel(q_ref, kt_ref, v_ref, bias_ref, o_ref, *, rows, heads):
    def first_key_row(r):
        return min(max(r - NA_KH // 2, 0), rows - NA_KH)

    kt_even = [kt_ref[h] for h in range(heads)]
    kt_odd = [jnp.concatenate([kt[:, GRID_W:], kt[:, :GRID_W]], axis=1) for kt in kt_even]
    units = [(h, r) for h in range(heads) for r in range(rows)]

    def scores(h, r):
        rs = first_key_row(r)
        k0 = (rs - rs % 2) * GRID_W
        kt = (kt_odd if rs % 2 else kt_even)[h][:, k0:k0 + NA_KH * GRID_W]
        q = q_ref[h, r * GRID_W:(r + 1) * GRID_W, :]
        s = jnp.dot(q, kt, preferred_element_type=F32) + bias_ref[h, r - rs]
        return s, jnp.max(s, axis=-1, keepdims=True)

    def finish(h, r, s_and_max):
        s, m = s_and_max
        rs = first_key_row(r)
        v = v_ref[h, rs * GRID_W:(rs + NA_KH) * GRID_W, :]
        p = jnp.exp2(s - m)
        denom = jnp.sum(p, axis=-1, keepdims=True)
        o = jnp.dot(p.astype(BF16), v, preferred_element_type=F32) / denom
        o_ref[h, r * GRID_W:(r + 1) * GRID_W, :] = o.astype(BF16)

    _issue_ahead(len(units), NBR_LOOKAHEAD, lambda u: scores(*units[u]), lambda u, s: finish(*units[u], s))


def _nbr_attn(qv_b, kt_b, b, seq, bias_b):
    rows = seq // GRID_W
    heads = max(1, NBR_ROWS_PER_STEP // rows)
    return pl.pallas_call(
        functools.partial(_nbr_attn_kernel, rows=rows, heads=heads),
        grid=(b, B_HEADS // heads),
        in_specs=[
            pl.BlockSpec((heads, seq, HEAD_DIM), lambda bi, h: (h, bi, 0)),
            pl.BlockSpec((heads, HEAD_DIM, seq), lambda bi, h: (h, 0, bi)),
            pl.BlockSpec((heads, seq, HEAD_DIM), lambda bi, h: (B_HEADS // heads + h, bi, 0)),
            pl.BlockSpec((heads, NA_KH, GRID_W, NA_KH * GRID_W), lambda bi, h: (h, 0, 0, 0)),
        ],
        out_specs=pl.BlockSpec((heads, seq, HEAD_DIM), lambda bi, h: (h, bi, 0)),
        out_shape=jax.ShapeDtypeStruct((B_HEADS, b * seq, HEAD_DIM), BF16),
        compiler_params=pltpu.CompilerParams(
            dimension_semantics=("parallel", "arbitrary"), vmem_limit_bytes=VMEM_LIMIT_BYTES),
        name="nbr_attn",
    )(qv_b, kt_b, qv_b, bias_b)


def _merge_kernel(oat_ref, ob_ref, ga_ref, gb_ref, x_ref, wa_ref, wb_ref, wo_ref, o_ref):
    merged = []
    for r0 in range(0, MERGE_TM, MERGE_SUB):
        rows = slice(r0, r0 + MERGE_SUB)
        ya = lax.dot_general(oat_ref[0, :, rows], wa_ref[...], (((0,), (0,)), ((), ())), preferred_element_type=F32)
        ob = jnp.concatenate([ob_ref[h, rows, :] for h in range(B_HEADS)], axis=1)
        yb = jnp.dot(ob, wb_ref[...], preferred_element_type=F32)
        gated = (jax.nn.sigmoid(ga_ref[rows, :].astype(F32)) * ya
                 + jax.nn.sigmoid(gb_ref[rows, :].astype(F32)) * yb)
        merged.append(gated.astype(BF16))
    for r0, mrg in zip(range(0, MERGE_TM, MERGE_SUB), merged):
        rows = slice(r0, r0 + MERGE_SUB)
        o_ref[rows, :] = (x_ref[rows, :] + jnp.dot(mrg, wo_ref[...], preferred_element_type=F32)).astype(BF16)


def _merge(oat, ob, proj, x2, wa, wb, wo):
    m = x2.shape[0]
    tiles_per_seq = oat.shape[2] // MERGE_TM
    resident = functools.partial(pl.BlockSpec, index_map=lambda i: (0, 0), pipeline_mode=pl.Buffered(1))
    return pl.pallas_call(
        _merge_kernel,
        grid=(m // MERGE_TM,),
        in_specs=[
            pl.BlockSpec((1, A_Q_W, MERGE_TM), lambda i: (i // tiles_per_seq, 0, i % tiles_per_seq)),
            pl.BlockSpec((B_HEADS, MERGE_TM, HEAD_DIM), lambda i: (0, i, 0)),
            pl.BlockSpec((MERGE_TM, D_MODEL), lambda i: (i, COL_GA // D_MODEL)),
            pl.BlockSpec((MERGE_TM, D_MODEL), lambda i: (i, COL_GB // D_MODEL)),
            pl.BlockSpec((MERGE_TM, D_MODEL), lambda i: (i, 0)),
            resident((A_Q_W, D_MODEL)),
            resident((B_W, D_MODEL)),
            resident((D_MODEL, D_MODEL)),
        ],
        out_specs=pl.BlockSpec((MERGE_TM, D_MODEL), lambda i: (i, 0)),
        out_shape=jax.ShapeDtypeStruct((m, D_MODEL), BF16),
        compiler_params=pltpu.CompilerParams(
            dimension_semantics=("parallel",), vmem_limit_bytes=VMEM_LIMIT_BYTES),
        name="merge",
    )(oat, ob, proj, proj, x2, wa, wb, wo)


def _mlp_kernel(x_ref, g_ref, wu_ref, wd_ref, o_ref, h_ref):
    @pl.when(pl.program_id(1) == 0)
    def _():
        x = x_ref[...].astype(F32)
        h_ref[...] = (x * _rms_scale(x) * g_ref[...]).astype(BF16)
        o_ref[...] = x

    u = jnp.dot(h_ref[...], wu_ref[...], preferred_element_type=F32)
    a = jnp.square(jnp.maximum(u, 0.0)).astype(BF16)
    o_ref[...] += jnp.dot(a, wd_ref[...], preferred_element_type=F32)


def _mlp(x2, g, wu, wd):
    m = x2.shape[0]
    return pl.pallas_call(
        _mlp_kernel,
        grid=(m // MLP_TM, D_FF // MLP_TF),
        in_specs=[
            pl.BlockSpec((MLP_TM, D_MODEL), lambda i, f: (i, 0)),
            pl.BlockSpec((1, D_MODEL), lambda i, f: (0, 0)),
            pl.BlockSpec((D_MODEL, MLP_TF), lambda i, f: (0, f)),
            pl.BlockSpec((MLP_TF, D_MODEL), lambda i, f: (f, 0)),
        ],
        out_specs=pl.BlockSpec((MLP_TM, D_MODEL), lambda i, f: (i, 0)),
        out_shape=jax.ShapeDtypeStruct((m, D_MODEL), F32),
        scratch_shapes=[pltpu.VMEM((MLP_TM, D_MODEL), BF16)],
        compiler_params=pltpu.CompilerParams(
            dimension_semantics=("parallel", "arbitrary"), vmem_limit_bytes=VMEM_LIMIT_BYTES),
        name="mlp",
    )(x2, g, wu, wd)


def _t5_bucket(rel):
    nb = N_BUCKETS // 2
    max_exact = nb // 2
    ret = (rel > 0).astype(np.int32) * nb
    n = np.abs(rel).astype(np.int32)
    nf = np.maximum(n, max_exact).astype(np.float32)
    large = max_exact + (np.log(nf / max_exact) / np.log(MAX_DISTANCE / max_exact) * (nb - max_exact)).astype(np.int32)
    large = np.minimum(large, nb - 1)
    return ret + np.where(n < max_exact, n, large)


def _select_rows(onehot, table):
    return jnp.einsum("ij,j...->i...", jnp.asarray(onehot, F32), table.astype(F32), precision=lax.Precision.HIGHEST)


def _window_bias_table(t5_bias):
    n_off = 4 * BLK
    offsets = np.arange(n_off - 1) - (2 * BLK - 1)
    onehot = _t5_bucket(offsets)[:, None] == np.arange(N_BUCKETS)[None, :]
    per_offset = jnp.where((np.abs(offsets) <= WINDOW)[:, None], _select_rows(onehot, t5_bias), NEG)
    y = jnp.pad(per_offset.T, ((0, 0), (0, 1)))
    skew = jnp.tile(y, (1, BLK + 1))[:, :BLK * (n_off + 1)].reshape(A_Q_HEADS, BLK, n_off + 1)
    table = skew[:, ::-1, :3 * BLK].reshape(A_KV_HEADS, A_GROUP, BLK, 3 * BLK)
    return jnp.transpose(table, (0, 3, 1, 2)).reshape(A_KV_HEADS, 3 * BLK, A_GROUP * BLK)


def _nbr_bias_table(rpb):
    qc = np.arange(GRID_W)[:, None]
    kc = np.arange(GRID_W)[None, :]
    start_c = np.clip(qc - NA_KW // 2, 0, GRID_W - NA_KW)
    col_mask = (kc >= start_c) & (kc < start_c + NA_KW)
    dcc = np.clip(kc - qc, -(NA_KW - 1), NA_KW - 1) + (NA_KW - 1)
    d = np.arange(NA_KH)[:, None]
    j = np.arange(NA_KH)[None, :]
    pick_row = (j - d + NA_KH - 1)[:, :, None] == np.arange(2 * NA_KH - 1)
    pick_col = dcc[:, :, None] == np.arange(2 * NA_KW - 1)
    exact = dict(precision=lax.Precision.HIGHEST)
    by_row = jnp.einsum("djr,hrw->hdjw", jnp.asarray(pick_row, F32), rpb.astype(F32), **exact)
    table = jnp.einsum("hdjw,ckw->hdcjk", by_row, jnp.asarray(pick_col, F32), **exact)
    table = jnp.where(col_mask[None, None, :, None, :], table, NEG)
    return table.reshape(B_HEADS, NA_KH, GRID_W, NA_KH * GRID_W)


def _prepare(norm_mix_g, w_in, q_norm_a, k_norm_a, t5_bias, sink_a, q_norm_b, k_norm_b, rpb_b,
             w_br_a, w_br_b, w_out, norm_mlp_g, w_up, w_down):
    scale = HEAD_DIM ** -0.5
    colgain = jnp.concatenate([
        jnp.tile(q_norm_a[0].astype(F32), A_Q_HEADS) * (scale * LOG2E),
        jnp.tile(k_norm_a[0].astype(F32), A_KV_HEADS),
        jnp.ones((A_KV_W,), F32),
        jnp.tile(q_norm_b[0].astype(F32), B_HEADS) * (scale * LOG2E),
        jnp.tile(k_norm_b[0].astype(F32), B_HEADS),
        jnp.ones((B_W + 2 * D_MODEL,), F32)]).reshape(1, IN_WIDTH)
    return dict(
        g_mix=norm_mix_g[0].astype(F32).reshape(1, D_MODEL),
        w_in=w_in[0].astype(BF16),
        colgain=colgain,
        sink=sink_a[0].astype(F32) * LOG2E,
        bias_a=_window_bias_table(t5_bias * LOG2E),
        bias_b=_nbr_bias_table(rpb_b[0] * LOG2E),
        w_a=w_br_a[0].astype(BF16),
        w_b=w_br_b[0].astype(BF16),
        w_o=w_out[0].astype(BF16),
        g_mlp=norm_mlp_g[0].astype(F32).reshape(1, D_MODEL),
        w_up=w_up[0].astype(BF16),
        w_down=w_down[0].astype(BF16),
    )


def _layer(x, p):
    b, seq, _ = x.shape
    x2 = x.reshape(b * seq, D_MODEL)
    proj, qv_b, kt_b = _in_proj(x2, p["g_mix"], p["w_in"], p["colgain"])
    proj3 = proj.reshape(b, seq, PROJ_W)
    oat = _win_attn(proj3, p["sink"], p["bias_a"])
    ob = _nbr_attn(qv_b, kt_b, b, seq, p["bias_b"])
    x1 = _merge(oat, ob, proj, x2, p["w_a"], p["w_b"], p["w_o"])
    y = _mlp(x1, p["g_mlp"], p["w_up"], p["w_down"])
    return y.reshape(b, seq, D_MODEL)


def kernel(x_prompt, x_sample, norm_mix_g, w_in, q_norm_a, k_norm_a, t5_bias, sink_a, q_norm_b, k_norm_b, rpb_b,
           w_br_a, w_br_b, w_out, norm_mlp_g, w_up, w_down):
    p = _prepare(norm_mix_g, w_in, q_norm_a, k_norm_a, t5_bias, sink_a, q_norm_b, k_norm_b, rpb_b,
                 w_br_a, w_br_b, w_out, norm_mlp_g, w_up, w_down)
    return (_layer(x_prompt, p), _layer(x_sample, p))
```

```python
import functools

import jax
import jax.numpy as jnp
import numpy as np
from jax import lax
from jax.experimental import pallas as pl
from jax.experimental.pallas import tpu as pltpu

D_MODEL = 2048
HEAD_DIM = 128
A_Q_HEADS = 8
A_KV_HEADS = 2
A_GROUP = A_Q_HEADS // A_KV_HEADS
WINDOW = 128
BLK = WINDOW
N_BUCKETS = 32
MAX_DISTANCE = 128
B_HEADS = 8
GRID_W = 64
NA_KH = 8
NA_KW = 16
D_FF = 4 * D_MODEL
EPS = 1e-6
NEG = -1e30
LOG2E = float(np.log2(np.e))

A_Q_W = A_Q_HEADS * HEAD_DIM
A_KV_W = A_KV_HEADS * HEAD_DIM
B_W = B_HEADS * HEAD_DIM
IN_WIDTH = A_Q_W + 2 * A_KV_W + 3 * B_W + 2 * D_MODEL

SRC_QA = 0
SRC_KVA = SRC_QA + A_Q_W
SRC_QB = SRC_KVA + 2 * A_KV_W
SRC_KB = SRC_QB + B_W
SRC_VB = SRC_KB + B_W
SRC_GA = SRC_VB + B_W
SRC_GB = SRC_GA + D_MODEL
assert SRC_GB + D_MODEL == IN_WIDTH

COL_GA = 0
COL_GB = COL_GA + D_MODEL
COL_QA = COL_GB + D_MODEL
COL_KVA = COL_QA + A_Q_W
PROJ_W = COL_KVA + 2 * A_KV_W

V7X_VMEM_BYTES = 64 * 1024 * 1024
VMEM_LIMIT_BYTES = V7X_VMEM_BYTES - 8 * 1024 * 1024

PROJ_TM = 256
PROJ_TN = 512
WIN_TQ = 16 * BLK
WIN_LOOKAHEAD = 1
NBR_LOOKAHEAD = 6
NBR_HEADS = 1
MERGE_TM = 512
MERGE_SUB = 256
MLP_TM = 1024
MLP_TF = 1024

BF16 = jnp.bfloat16
F32 = jnp.float32


def _rms_scale(x):
    return lax.rsqrt(jnp.mean(x * x, axis=-1, keepdims=True) + EPS)


def _issue_ahead(n, depth, first_stage, second_stage):
    pending = [first_stage(i) for i in range(min(depth, n))]
    for i in range(n):
        if i + depth < n:
            pending.append(first_stage(i + depth))
        second_stage(i, pending.pop(0))


def _head_rmsnorm(a):
    parts = []
    for c in range(a.shape[1] // HEAD_DIM):
        blk = a[:, c * HEAD_DIM:(c + 1) * HEAD_DIM]
        parts.append(blk * _rms_scale(blk))
    return jnp.concatenate(parts, axis=1)


def _in_proj_kernel(x_ref, g_ref, w_ref, cg_ref, o_ref, qvb_ref, kbt_ref):
    x = x_ref[...]
    h = (x * _rms_scale(x) * g_ref[...]).astype(BF16)
    for c0 in range(0, IN_WIDTH, PROJ_TN):
        cols = slice(c0, c0 + PROJ_TN)
        acc = jnp.dot(h, w_ref[:, cols], preferred_element_type=F32)
        if c0 < SRC_KVA or SRC_QB <= c0 < SRC_VB:
            acc = _head_rmsnorm(acc) * cg_ref[:, cols]
        elif c0 == SRC_KVA:
            k = _head_rmsnorm(acc[:, :A_KV_W]) * cg_ref[:, c0:c0 + A_KV_W]
            acc = jnp.concatenate([k, acc[:, A_KV_W:]], axis=1)
        if SRC_QB <= c0 < SRC_GA:
            for j in range(PROJ_TN // HEAD_DIM):
                head = acc[:, j * HEAD_DIM:(j + 1) * HEAD_DIM]
                if c0 < SRC_KB:
                    qvb_ref[(c0 - SRC_QB) // HEAD_DIM + j] = head.astype(BF16)
                elif c0 < SRC_VB:
                    kbt_ref[(c0 - SRC_KB) // HEAD_DIM + j] = head.T.astype(BF16)
                else:
                    qvb_ref[B_HEADS + (c0 - SRC_VB) // HEAD_DIM + j] = head.astype(BF16)
        else:
            if c0 < SRC_KVA:
                d0 = COL_QA + c0 - SRC_QA
            elif c0 == SRC_KVA:
                d0 = COL_KVA
            elif c0 < SRC_GB:
                d0 = COL_GA + c0 - SRC_GA
            else:
                d0 = COL_GB + c0 - SRC_GB
            o_ref[:, d0:d0 + PROJ_TN] = acc.astype(BF16)


def _in_proj(x2, g, w, colgain):
    m = x2.shape[0]
    resident = functools.partial(pl.BlockSpec, index_map=lambda i: (0, 0), pipeline_mode=pl.Buffered(1))
    return pl.pallas_call(
        _in_proj_kernel,
        grid=(m // PROJ_TM,),
        in_specs=[
            pl.BlockSpec((PROJ_TM, D_MODEL), lambda i: (i, 0)),
            resident((1, D_MODEL)),
            resident((D_MODEL, IN_WIDTH)),
            resident((1, IN_WIDTH)),
        ],
        out_specs=[
            pl.BlockSpec((PROJ_TM, PROJ_W), lambda i: (i, 0)),
            pl.BlockSpec((2 * B_HEADS, PROJ_TM, HEAD_DIM), lambda i: (0, i, 0)),
            pl.BlockSpec((B_HEADS, HEAD_DIM, PROJ_TM), lambda i: (0, 0, i)),
        ],
        out_shape=[
            jax.ShapeDtypeStruct((m, PROJ_W), BF16),
            jax.ShapeDtypeStruct((2 * B_HEADS, m, HEAD_DIM), BF16),
            jax.ShapeDtypeStruct((B_HEADS, HEAD_DIM, m), BF16),
        ],
        compiler_params=pltpu.CompilerParams(
            dimension_semantics=("parallel",), vmem_limit_bytes=VMEM_LIMIT_BYTES),
        name="in_proj",
    )(x2, g, w, colgain)


def _win_attn_kernel(sink_ref, q_ref, kvp_ref, kvm_ref, kvn_ref, bias_ref, o_ref, *, n_tiles):
    i = pl.program_id(1)
    kv = jnp.concatenate([kvp_ref[0], kvm_ref[0], kvn_ref[0]], axis=0)
    key = lax.broadcasted_iota(jnp.int32, (3 * BLK, 1), 0)
    first_key = jnp.where(i == 0, BLK, 0)
    end_key = jnp.where(i == n_tiles - 1, 2 * BLK, 3 * BLK)
    n_blk = WIN_TQ // BLK
    units = [(t, kvh) for t in range(n_blk) for kvh in range(A_KV_HEADS)]

    def scores(t, kvh):
        k = kv[t * BLK:(t + 3) * BLK, kvh * HEAD_DIM:(kvh + 1) * HEAD_DIM]
        q = jnp.concatenate(
            [q_ref[0, t * BLK:(t + 1) * BLK, (kvh * A_GROUP + g) * HEAD_DIM:(kvh * A_GROUP + g + 1) * HEAD_DIM]
             for g in range(A_GROUP)], axis=0)
        s = lax.dot_general(k, q, (((1,), (1,)), ((), ())), preferred_element_type=F32)
        s = s + bias_ref[kvh]
        if t == 0:
            s = jnp.where(key >= first_key, s, NEG)
        if t == n_blk - 1:
            s = jnp.where(key < end_key, s, NEG)
        return s, jnp.max(s, axis=0, keepdims=True)

    def finish(t, kvh, s_and_max):
        s, col_max = s_and_max
        heads = [kvh * A_GROUP + g for g in range(A_GROUP)]
        v = kv[t * BLK:(t + 3) * BLK, A_KV_W + kvh * HEAD_DIM:A_KV_W + (kvh + 1) * HEAD_DIM]
        sink = jnp.concatenate([jnp.full((1, BLK), sink_ref[h], F32) for h in heads], axis=1)
        m = jnp.maximum(col_max, sink)
        p = jnp.exp2(s - m)
        denom = jnp.sum(p, axis=0, keepdims=True) + jnp.exp2(sink - m)
        o = lax.dot_general(v, p.astype(BF16), (((0,), (0,)), ((), ())), preferred_element_type=F32) / denom
        for g, h in enumerate(heads):
            o_ref[0, h * HEAD_DIM:(h + 1) * HEAD_DIM, t * BLK:(t + 1) * BLK] = (
                o[:, g * BLK:(g + 1) * BLK].astype(BF16))

    _issue_ahead(len(units), WIN_LOOKAHEAD, lambda u: scores(*units[u]), lambda u, s: finish(*units[u], s))


def _win_attn(proj3, sink, bias_a):
    b, seq, _ = proj3.shape
    n_tiles = seq // WIN_TQ
    blk_per_tile = WIN_TQ // BLK
    last_blk = seq // BLK - 1
    kva_wide = COL_KVA // (2 * A_KV_W)
    return pl.pallas_call(
        functools.partial(_win_attn_kernel, n_tiles=n_tiles),
        grid=(b, n_tiles),
        in_specs=[
            pl.BlockSpec(memory_space=pltpu.SMEM),
            pl.BlockSpec((1, WIN_TQ, A_Q_W), lambda bi, i: (bi, i, COL_QA // A_Q_W)),
            pl.BlockSpec((1, BLK, 2 * A_KV_W),
                         lambda bi, i: (bi, jnp.maximum(i * blk_per_tile - 1, 0), kva_wide)),
            pl.BlockSpec((1, WIN_TQ, 2 * A_KV_W), lambda bi, i: (bi, i, kva_wide)),
            pl.BlockSpec((1, BLK, 2 * A_KV_W),
                         lambda bi, i: (bi, jnp.minimum((i + 1) * blk_per_tile, last_blk), kva_wide)),
            pl.BlockSpec((A_KV_HEADS, 3 * BLK, A_GROUP * BLK), lambda bi, i: (0, 0, 0)),
        ],
        out_specs=pl.BlockSpec((1, A_Q_W, WIN_TQ), lambda bi, i: (bi, 0, i)),
        out_shape=jax.ShapeDtypeStruct((b, A_Q_W, seq), BF16),
        compiler_params=pltpu.CompilerParams(
            dimension_semantics=("parallel", "arbitrary"), vmem_limit_bytes=VMEM_LIMIT_BYTES),
        name="win_attn",
    )(sink, proj3, proj3, proj3, proj3, bias_a)


def _nbr_attn_kernel(q_ref, kt_ref, v_ref, bias_ref, *refs, rows, n_cast):
    o_ref = refs[n_cast]
    for src, dst in zip(refs[:n_cast], refs[n_cast + 1:]):
        dst[...] = src[...].astype(BF16)

    def first_key_row(r):
        return min(max(r - NA_KH // 2, 0), rows - NA_KH)

    kt_even = [kt_ref[h] for h in range(NBR_HEADS)]
    kt_odd = [jnp.concatenate([kt[:, GRID_W:], kt[:, :GRID_W]], axis=1) for kt in kt_even]
    units = [(h, r) for h in range(NBR_HEADS) for r in range(rows)]

    def scores(h, r):
        rs = first_key_row(r)
        k0 = (rs - rs % 2) * GRID_W
        kt = (kt_odd if rs % 2 else kt_even)[h][:, k0:k0 + NA_KH * GRID_W]
        q = q_ref[h, r * GRID_W:(r + 1) * GRID_W, :]
        s = jnp.dot(q, kt, preferred_element_type=F32) + bias_ref[h, r - rs]
        return s, jnp.max(s, axis=-1, keepdims=True)

    def finish(h, r, s_and_max):
        s, m = s_and_max
        rs = first_key_row(r)
        v = v_ref[h, rs * GRID_W:(rs + NA_KH) * GRID_W, :]
        p = jnp.exp2(s - m)
        denom = jnp.sum(p, axis=-1, keepdims=True)
        o = jnp.dot(p.astype(BF16), v, preferred_element_type=F32) / denom
        o_ref[h, r * GRID_W:(r + 1) * GRID_W, :] = o.astype(BF16)

    _issue_ahead(len(units), NBR_LOOKAHEAD, lambda u: scores(*units[u]), lambda u, s: finish(*units[u], s))


def _nbr_attn(qv_b, kt_b, b, seq, bias_b, cast_f32=()):
    rows = seq // GRID_W
    n_h = B_HEADS // NBR_HEADS
    n_steps = b * n_h

    def slab_spec(w):
        return pl.BlockSpec((w.shape[0] // n_steps, w.shape[1]), lambda bi, h: (bi * n_h + h, 0))

    for w in cast_f32:
        assert w.shape[0] % (n_steps * 16) == 0, w.shape
    outs = pl.pallas_call(
        functools.partial(_nbr_attn_kernel, rows=rows, n_cast=len(cast_f32)),
        grid=(b, n_h),
        in_specs=[
            pl.BlockSpec((NBR_HEADS, seq, HEAD_DIM), lambda bi, h: (h, bi, 0)),
            pl.BlockSpec((NBR_HEADS, HEAD_DIM, seq), lambda bi, h: (h, 0, bi)),
            pl.BlockSpec((NBR_HEADS, seq, HEAD_DIM), lambda bi, h: (n_h + h, bi, 0)),
            pl.BlockSpec((NBR_HEADS, NA_KH, GRID_W, NA_KH * GRID_W), lambda bi, h: (h, 0, 0, 0)),
        ] + [slab_spec(w) for w in cast_f32],
        out_specs=[pl.BlockSpec((NBR_HEADS, seq, HEAD_DIM), lambda bi, h: (h, bi, 0))]
        + [slab_spec(w) for w in cast_f32],
        out_shape=[jax.ShapeDtypeStruct((B_HEADS, b * seq, HEAD_DIM), BF16)]
        + [jax.ShapeDtypeStruct(w.shape, BF16) for w in cast_f32],
        compiler_params=pltpu.CompilerParams(
            dimension_semantics=("parallel", "arbitrary"), vmem_limit_bytes=VMEM_LIMIT_BYTES),
        name="nbr_attn",
    )(qv_b, kt_b, qv_b, bias_b, *cast_f32)
    return outs[0], tuple(outs[1:])


def _merge_kernel(oat_ref, ob_ref, ga_ref, gb_ref, x_ref, wa_ref, wb_ref, wo_ref, o_ref):
    merged = []
    for r0 in range(0, MERGE_TM, MERGE_SUB):
        rows = slice(r0, r0 + MERGE_SUB)
        ya = lax.dot_general(oat_ref[0, :, rows], wa_ref[...], (((0,), (0,)), ((), ())), preferred_element_type=F32)
        ob = jnp.concatenate([ob_ref[h, rows, :] for h in range(B_HEADS)], axis=1)
        yb = jnp.dot(ob, wb_ref[...], preferred_element_type=F32)
        gated = (jax.nn.sigmoid(ga_ref[rows, :].astype(F32)) * ya
                 + jax.nn.sigmoid(gb_ref[rows, :].astype(F32)) * yb)
        merged.append(gated.astype(BF16))
    for r0, mrg in zip(range(0, MERGE_TM, MERGE_SUB), merged):
        rows = slice(r0, r0 + MERGE_SUB)
        o_ref[rows, :] = (x_ref[rows, :] + jnp.dot(mrg, wo_ref[...], preferred_element_type=F32)).astype(BF16)


def _merge(oat, ob, proj, x2, wa, wb, wo):
    m = x2.shape[0]
    tiles_per_seq = oat.shape[2] // MERGE_TM
    resident = functools.partial(pl.BlockSpec, index_map=lambda i: (0, 0), pipeline_mode=pl.Buffered(1))
    return pl.pallas_call(
        _merge_kernel,
        grid=(m // MERGE_TM,),
        in_specs=[
            pl.BlockSpec((1, A_Q_W, MERGE_TM), lambda i: (i // tiles_per_seq, 0, i % tiles_per_seq)),
            pl.BlockSpec((B_HEADS, MERGE_TM, HEAD_DIM), lambda i: (0, i, 0)),
            pl.BlockSpec((MERGE_TM, D_MODEL), lambda i: (i, COL_GA // D_MODEL)),
            pl.BlockSpec((MERGE_TM, D_MODEL), lambda i: (i, COL_GB // D_MODEL)),
            pl.BlockSpec((MERGE_TM, D_MODEL), lambda i: (i, 0)),
            resident((A_Q_W, D_MODEL)),
            resident((B_W, D_MODEL)),
            resident((D_MODEL, D_MODEL)),
        ],
        out_specs=pl.BlockSpec((MERGE_TM, D_MODEL), lambda i: (i, 0)),
        out_shape=jax.ShapeDtypeStruct((m, D_MODEL), BF16),
        compiler_params=pltpu.CompilerParams(
            dimension_semantics=("parallel",), vmem_limit_bytes=VMEM_LIMIT_BYTES),
        name="merge",
    )(oat, ob, proj, proj, x2, wa, wb, wo)


def _mlp_kernel(x_ref, g_ref, wu_ref, wd_ref, o_ref, h_ref):
    @pl.when(pl.program_id(1) == 0)
    def _():
        x = x_ref[...].astype(F32)
        h_ref[...] = (x * _rms_scale(x) * g_ref[...]).astype(BF16)
        o_ref[...] = x

    u = jnp.dot(h_ref[...], wu_ref[...], preferred_element_type=F32)
    a = jnp.square(jnp.maximum(u, 0.0)).astype(BF16)
    o_ref[...] += jnp.dot(a, wd_ref[...], preferred_element_type=F32)


def _mlp(x2, g, wu, wd):
    m = x2.shape[0]
    return pl.pallas_call(
        _mlp_kernel,
        grid=(m // MLP_TM, D_FF // MLP_TF),
        in_specs=[
            pl.BlockSpec((MLP_TM, D_MODEL), lambda i, f: (i, 0)),
            pl.BlockSpec((1, D_MODEL), lambda i, f: (0, 0)),
            pl.BlockSpec((D_MODEL, MLP_TF), lambda i, f: (0, f)),
            pl.BlockSpec((MLP_TF, D_MODEL), lambda i, f: (f, 0)),
        ],
        out_specs=pl.BlockSpec((MLP_TM, D_MODEL), lambda i, f: (i, 0)),
        out_shape=jax.ShapeDtypeStruct((m, D_MODEL), F32),
        scratch_shapes=[pltpu.VMEM((MLP_TM, D_MODEL), BF16)],
        compiler_params=pltpu.CompilerParams(
            dimension_semantics=("parallel", "arbitrary"), vmem_limit_bytes=VMEM_LIMIT_BYTES),
        name="mlp",
    )(x2, g, wu, wd)


def _t5_bucket(rel):
    nb = N_BUCKETS // 2
    max_exact = nb // 2
    ret = (rel > 0).astype(np.int32) * nb
    n = np.abs(rel).astype(np.int32)
    nf = np.maximum(n, max_exact).astype(np.float32)
    large = max_exact + (np.log(nf / max_exact) / np.log(MAX_DISTANCE / max_exact) * (nb - max_exact)).astype(np.int32)
    large = np.minimum(large, nb - 1)
    return ret + np.where(n < max_exact, n, large)


def _select_rows(onehot, table):
    return jnp.einsum("ij,j...->i...", jnp.asarray(onehot, F32), table.astype(F32), precision=lax.Precision.HIGHEST)


def _window_bias_table(t5_bias):
    n_off = 4 * BLK
    offsets = np.arange(n_off - 1) - (2 * BLK - 1)
    onehot = _t5_bucket(offsets)[:, None] == np.arange(N_BUCKETS)[None, :]
    per_offset = jnp.where((np.abs(offsets) <= WINDOW)[:, None], _select_rows(onehot, t5_bias), NEG)
    y = jnp.pad(per_offset.T, ((0, 0), (0, 1)))
    skew = jnp.tile(y, (1, BLK + 1))[:, :BLK * (n_off + 1)].reshape(A_Q_HEADS, BLK, n_off + 1)
    table = skew[:, ::-1, :3 * BLK].reshape(A_KV_HEADS, A_GROUP, BLK, 3 * BLK)
    return jnp.transpose(table, (0, 3, 1, 2)).reshape(A_KV_HEADS, 3 * BLK, A_GROUP * BLK)


def _nbr_bias_table(rpb):
    qc = np.arange(GRID_W)[:, None]
    kc = np.arange(GRID_W)[None, :]
    start_c = np.clip(qc - NA_KW // 2, 0, GRID_W - NA_KW)
    col_mask = (kc >= start_c) & (kc < start_c + NA_KW)
    dcc = np.clip(kc - qc, -(NA_KW - 1), NA_KW - 1) + (NA_KW - 1)
    d = np.arange(NA_KH)[:, None]
    j = np.arange(NA_KH)[None, :]
    pick_row = (j - d + NA_KH - 1)[:, :, None] == np.arange(2 * NA_KH - 1)
    pick_col = dcc[:, :, None] == np.arange(2 * NA_KW - 1)
    exact = dict(precision=lax.Precision.HIGHEST)
    by_row = jnp.einsum("djr,hrw->hdjw", jnp.asarray(pick_row, F32), rpb.astype(F32), **exact)
    table = jnp.einsum("hdjw,ckw->hdcjk", by_row, jnp.asarray(pick_col, F32), **exact)
    table = jnp.where(col_mask[None, None, :, None, :], table, NEG)
    return table.reshape(B_HEADS, NA_KH, GRID_W, NA_KH * GRID_W)


def _prepare(norm_mix_g, w_in, q_norm_a, k_norm_a, t5_bias, sink_a, q_norm_b, k_norm_b, rpb_b,
             w_br_a, w_br_b, w_out, norm_mlp_g, w_up, w_down):
    scale = HEAD_DIM ** -0.5
    colgain = jnp.concatenate([
        jnp.tile(q_norm_a[0].astype(F32), A_Q_HEADS) * (scale * LOG2E),
        jnp.tile(k_norm_a[0].astype(F32), A_KV_HEADS),
        jnp.ones((A_KV_W,), F32),
        jnp.tile(q_norm_b[0].astype(F32), B_HEADS) * (scale * LOG2E),
        jnp.tile(k_norm_b[0].astype(F32), B_HEADS),
        jnp.ones((B_W + 2 * D_MODEL,), F32)]).reshape(1, IN_WIDTH)
    return dict(
        g_mix=norm_mix_g[0].astype(F32).reshape(1, D_MODEL),
        w_in=w_in[0].astype(BF16),
        colgain=colgain,
        sink=sink_a[0].astype(F32) * LOG2E,
        bias_a=_window_bias_table(t5_bias * LOG2E),
        bias_b=_nbr_bias_table(rpb_b[0] * LOG2E),
        w_a=w_br_a[0].astype(BF16),
        w_b=w_br_b[0].astype(BF16),
        w_o=w_out[0].astype(BF16),
        g_mlp=norm_mlp_g[0].astype(F32).reshape(1, D_MODEL),
        mlp_w_f32=(w_up[0], w_down[0]),
    )


def _layer(x, p):
    b, seq, _ = x.shape
    x2 = x.reshape(b * seq, D_MODEL)
    proj, qv_b, kt_b = _in_proj(x2, p["g_mix"], p["w_in"], p["colgain"])
    proj3 = proj.reshape(b, seq, PROJ_W)
    oat = _win_attn(proj3, p["sink"], p["bias_a"])
    if "mlp_w" in p:
        ob, _ = _nbr_attn(qv_b, kt_b, b, seq, p["bias_b"])
    else:
        ob, p["mlp_w"] = _nbr_attn(qv_b, kt_b, b, seq, p["bias_b"], cast_f32=p["mlp_w_f32"])
    x1 = _merge(oat, ob, proj, x2, p["w_a"], p["w_b"], p["w_o"])
    y = _mlp(x1, p["g_mlp"], *p["mlp_w"])
    return y.reshape(b, seq, D_MODEL)


def kernel(x_prompt, x_sample, norm_mix_g, w_in, q_norm_a, k_norm_a, t5_bias, sink_a, q_norm_b, k_norm_b, rpb_b,
           w_br_a, w_br_b, w_out, norm_mlp_g, w_up, w_down):
    p = _prepare(norm_mix_g, w_in, q_norm_a, k_norm_a, t5_bias, sink_a, q_norm_b, k_norm_b, rpb_b,
                 w_br_a, w_br_b, w_out, norm_mlp_g, w_up, w_down)
    y_prompt = _layer(x_prompt, p)
    return (y_prompt, _layer(x_sample, p))
```
